```python
import math
import jax, jax.numpy as jnp
from jax import lax
import numpy as np

D_MODEL = 1024
BATCH = 8
SEQ = 4096
DEPTH = 4

MIX_WIDTH = D_MODEL
N_MIXERS = 4
GROUP_WIDTH = MIX_WIDTH // N_MIXERS
ATT_HEADS = 4
ATT_VDIM = GROUP_WIDTH // ATT_HEADS
ATT_QK = ATT_VDIM // 2
Q_BLOCK = 128
CONV_WIDTH = 31
CONV_GROUPS = 4
FNET_GROUPS = 4
SGU_CHUNK = 128
SGU_GROUPS = 4
D_FF = 4 * D_MODEL
EPS = 1e-6
A_COLS = 3 * GROUP_WIDTH
B_COLS = 2 * GROUP_WIDTH
C_COLS = GROUP_WIDTH
D_COLS = 2 * GROUP_WIDTH
IN_COLS = A_COLS + B_COLS + C_COLS + D_COLS

kernel_name = "hybrid_parallel_mixer_encoder"


def _rmsnorm(x, g):
    xf = x.astype(jnp.float32)
    y = xf * lax.rsqrt(jnp.mean(xf * xf, axis=-1, keepdims=True) + EPS)
    return (y * g.astype(jnp.float32)).astype(x.dtype)


def _layernorm(x, g, b):
    xf = x.astype(jnp.float32)
    mu = jnp.mean(xf, axis=-1, keepdims=True)
    xc = xf - mu
    var = jnp.mean(xc * xc, axis=-1, keepdims=True)
    y = xc * lax.rsqrt(var + EPS) * g.astype(jnp.float32) + b.astype(jnp.float32)
    return y.astype(x.dtype)


def _alibi_slopes(n):
    return jnp.asarray([2.0 ** (-8.0 * (i + 1) / n) for i in range(n)], dtype=jnp.float32)


def _diff_attention(q, k, v, lam, lam_init, subln_g):
    bsz, seq = q.shape[0], q.shape[1]
    nb = seq // Q_BLOCK
    scale = ATT_QK ** -0.5
    slopes = _alibi_slopes(ATT_HEADS)
    kpos = jnp.arange(seq, dtype=jnp.float32)
    qb = q.reshape(bsz, nb, Q_BLOCK, ATT_HEADS, 2, ATT_QK).transpose(1, 0, 2, 3, 4, 5)

    def block(args):
        q_blk, i = args
        qpos = (i * Q_BLOCK + jnp.arange(Q_BLOCK)).astype(jnp.float32)
        bias = -slopes[:, None, None] * jnp.abs(qpos[:, None] - kpos[None, :])
        s = jnp.einsum('bqhmd,bkhmd->bhmqk', q_blk, k,
                       preferred_element_type=jnp.float32) * scale + bias[None, :, None]
        p = jax.nn.softmax(s, axis=-1)
        w = p[:, :, 0] - lam * p[:, :, 1]
        return jnp.einsum('bhqk,bkhe->bqhe', w.astype(v.dtype), v)

    o = lax.map(block, (qb, jnp.arange(nb)))
    o = o.transpose(1, 0, 2, 3, 4).reshape(bsz, seq, ATT_HEADS, ATT_VDIM)
    o = _rmsnorm(o, subln_g) * (1.0 - lam_init)
    return o.reshape(bsz, seq, GROUP_WIDTH)


def _conformer_conv(h, dw_w, dw_b, ln_g, ln_b, pw_w, pw_b):
    a, g = jnp.split(h, 2, axis=-1)
    z = a * jax.nn.sigmoid(g)
    z = lax.conv_general_dilated(
        z, dw_w[:, None, :], window_strides=(1,),
        padding=[(CONV_WIDTH // 2, CONV_WIDTH // 2)],
        dimension_numbers=('NWC', 'WIO', 'NWC'),
        feature_group_count=GROUP_WIDTH) + dw_b
    zg = z.reshape(z.shape[0], z.shape[1], CONV_GROUPS, GROUP_WIDTH // CONV_GROUPS)
    zg = _layernorm(zg, ln_g.reshape(CONV_GROUPS, -1), ln_b.reshape(CONV_GROUPS, -1))
    z = jax.nn.silu(zg.reshape(z.shape))
    return z @ pw_w + pw_b


def _fourier_mix(c, w, b):
    bsz, seq = c.shape[0], c.shape[1]
    cg = c.reshape(bsz, seq, FNET_GROUPS, GROUP_WIDTH // FNET_GROUPS).astype(jnp.float32)
    f = jnp.fft.fft2(cg, axes=(1, 3), norm='ortho').real.astype(c.dtype)
    return jnp.einsum('bsgc,gce->bsge', f, w).reshape(bsz, seq, GROUP_WIDTH) + b


def _spatial_gate(h, ln_g, ln_b, w_s, b_s):
    bsz, seq = h.shape[0], h.shape[1]
    u, v = jnp.split(h, 2, axis=-1)
    v = _layernorm(v, ln_g, ln_b)
    nc = seq // SGU_CHUNK
    vg = v.reshape(bsz, nc, SGU_CHUNK, SGU_GROUPS, GROUP_WIDTH // SGU_GROUPS)
    sv = jnp.einsum('gts,bnsgc->bntgc', w_s, vg) + b_s.T[None, None, :, :, None]
    return u * sv.reshape(bsz, seq, GROUP_WIDTH)


def setup_inputs(seed: int = 0) -> dict:
    key = jax.random.key(seed)
    ks = jax.random.split(key, 26)

    def nrm(k, shape, scale):
        return jax.random.normal(k, shape, jnp.float32) * scale

    def gain(k, shape):
        return 1.0 + 0.02 * jax.random.normal(k, shape, jnp.float32)

    gw = GROUP_WIDTH
    return {
        "x": nrm(ks[0], (BATCH, SEQ, D_MODEL), 1.0),
        "norm1_g": gain(ks[1], (DEPTH, D_MODEL)),
        "w_in": nrm(ks[2], (DEPTH, D_MODEL, IN_COLS), D_MODEL ** -0.5),
        "lam_q1": nrm(ks[3], (DEPTH, ATT_QK), 0.1),
        "lam_k1": nrm(ks[4], (DEPTH, ATT_QK), 0.1),
        "lam_q2": nrm(ks[5], (DEPTH, ATT_QK), 0.1),
        "lam_k2": nrm(ks[6], (DEPTH, ATT_QK), 0.1),
        "subln_g": gain(ks[7], (DEPTH, ATT_VDIM)),
        "conv_dw_w": nrm(ks[8], (DEPTH, CONV_WIDTH, gw), CONV_WIDTH ** -0.5),
        "conv_dw_b": nrm(ks[9], (DEPTH, gw), 0.02),
        "conv_ln_g": gain(ks[10], (DEPTH, gw)),
        "conv_ln_b": nrm(ks[11], (DEPTH, gw), 0.02),
        "conv_pw_w": nrm(ks[12], (DEPTH, gw, gw), gw ** -0.5),
        "conv_pw_b": nrm(ks[13], (DEPTH, gw), 0.02),
        "fnet_w": nrm(ks[14], (DEPTH, FNET_GROUPS, gw // FNET_GROUPS, gw // FNET_GROUPS),
                      (gw // FNET_GROUPS) ** -0.5),
        "fnet_b": nrm(ks[15], (DEPTH, gw), 0.02),
        "sgu_ln_g": gain(ks[16], (DEPTH, gw)),
        "sgu_ln_b": nrm(ks[17], (DEPTH, gw), 0.02),
        "sgu_w": nrm(ks[18], (DEPTH, SGU_GROUPS, SGU_CHUNK, SGU_CHUNK), SGU_CHUNK ** -0.5),
        "sgu_b": gain(ks[19], (DEPTH, SGU_GROUPS, SGU_CHUNK)),
        "w_out": nrm(ks[20], (DEPTH, MIX_WIDTH, D_MODEL), MIX_WIDTH ** -0.5),
        "norm2_g": gain(ks[21], (DEPTH, D_MODEL)),
        "w_up": nrm(ks[22], (DEPTH, D_MODEL, D_FF), D_MODEL ** -0.5),
        "w_down": nrm(ks[23], (DEPTH, D_FF, D_MODEL), D_FF ** -0.5),
        "final_g": gain(ks[24], (D_MODEL,)),
    }


def reference(x, norm1_g, w_in, lam_q1, lam_k1, lam_q2, lam_k2, subln_g,
              conv_dw_w, conv_dw_b, conv_ln_g, conv_ln_b, conv_pw_w, conv_pw_b,
              fnet_w, fnet_b, sgu_ln_g, sgu_ln_b, sgu_w, sgu_b,
              w_out, norm2_g, w_up, w_down, final_g):
    bsz, seq = x.shape[0], x.shape[1]
    splits = [GROUP_WIDTH, 2 * GROUP_WIDTH, A_COLS, A_COLS + B_COLS, A_COLS + B_COLS + C_COLS]
    for l in range(DEPTH):
        xn = _rmsnorm(x, norm1_g[l])
        h = xn @ w_in[l]
        hq, hk, hv, hb, hc, hd = jnp.split(h, splits, axis=-1)
        q = hq.reshape(bsz, seq, ATT_HEADS, 2, ATT_QK)
        k = hk.reshape(bsz, seq, ATT_HEADS, 2, ATT_QK)
        v = hv.reshape(bsz, seq, ATT_HEADS, ATT_VDIM)
        lam_init = 0.8 - 0.6 * math.exp(-0.3 * l)
        lam = (jnp.exp(jnp.sum((lam_q1[l] * lam_k1[l]).astype(jnp.float32)))
               - jnp.exp(jnp.sum((lam_q2[l] * lam_k2[l]).astype(jnp.float32)))
               + lam_init)
        y_a = _diff_attention(q, k, v, lam, lam_init, subln_g[l])
        y_b = _conformer_conv(hb, conv_dw_w[l], conv_dw_b[l], conv_ln_g[l], conv_ln_b[l],
                              conv_pw_w[l], conv_pw_b[l])
        y_c = _fourier_mix(hc, fnet_w[l], fnet_b[l])
        y_d = _spatial_gate(hd, sgu_ln_g[l], sgu_ln_b[l], sgu_w[l], sgu_b[l])
        y = jnp.concatenate([y_a, y_b, y_c, y_d], axis=-1)
        x = x + y @ w_out[l]
        xn2 = _rmsnorm(x, norm2_g[l])
        x = x + jnp.square(jax.nn.relu(xn2 @ w_up[l])) @ w_down[l]
    return _rmsnorm(x, final_g)
```

```python
import functools
import math

import numpy as np
import jax
import jax.numpy as jnp
from jax import lax
from jax.experimental import pallas as pl
from jax.experimental.pallas import tpu as pltpu

D_MODEL = 1024
GROUP_WIDTH = 256
ATT_HEADS = 4
ATT_VDIM = 64
ATT_QK = 32
CONV_WIDTH = 31
CONV_GROUPS = 4
FNET_GROUPS = 4
SGU_CHUNK = 128
SGU_GROUPS = 4
D_FF = 4 * D_MODEL
EPS = 1e-6
LOG2E = 1.4426950408889634

KPAD = 128
POS_LANE = 2 * ATT_QK
N_SPLIT = 3
TQ = 256
TK = 256
V_ROWS = ATT_VDIM + 16
TM = 512
VMEM_LIMIT = 56 * 1024 * 1024

BF16 = jnp.bfloat16
F32 = jnp.float32


def _params(sem, vmem=VMEM_LIMIT):
    return pltpu.CompilerParams(dimension_semantics=sem, vmem_limit_bytes=vmem)


def _full(shape):
    n = len(shape)
    return pl.BlockSpec(shape, lambda *_: (0,) * n)


def _inproj_kernel(x_ref, g_ref, wn_ref, wt_ref,
                   k_ref, hb_ref, hc_ref, u_ref, v_ref, qt_ref, vt_ref):
    x = x_ref[0]
    ms = jnp.mean(x * x, axis=-1, keepdims=True)
    xn = (x * lax.rsqrt(ms + EPS) * g_ref[...]).astype(BF16)
    nat = jnp.dot(xn, wn_ref[...], preferred_element_type=F32)
    tr = lax.dot_general(wt_ref[...], xn, (((1,), (1,)), ((), ())),
                         preferred_element_type=F32)
    kw = ATT_HEADS * KPAD
    k_ref[0] = nat[:, :kw].astype(BF16)
    hb_ref[0] = nat[:, kw:kw + 2 * GROUP_WIDTH]
    o = kw + 2 * GROUP_WIDTH
    hc_ref[0] = nat[:, o:o + GROUP_WIDTH].astype(BF16)
    u_ref[0] = nat[:, o + GROUP_WIDTH:o + 2 * GROUP_WIDTH]
    v_ref[0] = nat[:, o + 2 * GROUP_WIDTH:o + 3 * GROUP_WIDTH]
    qscale = (ATT_QK ** -0.5) * LOG2E
    for j in range(TM // TQ):
        qt_ref[0, j] = (tr[:kw, j * TQ:(j + 1) * TQ] * qscale).astype(BF16)
    for j in range(TM // TK):
        vt_ref[0, j] = tr[kw:, j * TK:(j + 1) * TK].astype(BF16)


def _inproj(x, g, wn, wt):
    b, s, d = x.shape
    kw = ATT_HEADS * KPAD
    nn = wn.shape[1]
    nt = wt.shape[0]
    grid = (b, s // TM)
    tok = lambda w: pl.BlockSpec((1, TM, w), lambda i, j: (i, j, 0))
    out_shape = (
        jax.ShapeDtypeStruct((b, s, kw), BF16),
        jax.ShapeDtypeStruct((b, s, 2 * GROUP_WIDTH), F32),
        jax.ShapeDtypeStruct((b, s, GROUP_WIDTH), BF16),
        jax.ShapeDtypeStruct((b, s, GROUP_WIDTH), F32),
        jax.ShapeDtypeStruct((b, s, GROUP_WIDTH), F32),
        jax.ShapeDtypeStruct((b, s // TQ, kw, TQ), BF16),
        jax.ShapeDtypeStruct((b, s // TK, GROUP_WIDTH, TK), BF16),
    )
    out_specs = (
        tok(kw), tok(2 * GROUP_WIDTH), tok(GROUP_WIDTH), tok(GROUP_WIDTH), tok(GROUP_WIDTH),
        pl.BlockSpec((1, TM // TQ, kw, TQ), lambda i, j: (i, j, 0, 0)),
        pl.BlockSpec((1, TM // TK, GROUP_WIDTH, TK), lambda i, j: (i, j, 0, 0)),
    )
    return pl.pallas_call(
        _inproj_kernel,
        out_shape=out_shape,
        grid=grid,
        in_specs=[tok(d), _full((1, d)), _full((d, nn)), _full((nt, d))],
        out_specs=out_specs,
        compiler_params=_params(("parallel", "parallel")),
        name="inproj",
    )(x, g, wn, wt)


def _attn_kernel(lam_ref, g_ref, crow_ref, posk_ref, posq_ref, dbias_ref,
                 qt_ref, k_ref, vt_ref, o_ref, kaug_ref, vaug_ref):
    nq = qt_ref.shape[1]
    nk = vt_ref.shape[1]

    for j in range(nk):
        kaug_ref[j] = k_ref[0, j * TK:(j + 1) * TK, :] + posk_ref[0]
        vaug_ref[j, :ATT_VDIM, :] = vt_ref[0, j]
        r = lax.broadcasted_iota(jnp.int32, (V_ROWS - ATT_VDIM, TK), 0)
        vaug_ref[j, ATT_VDIM:, :] = jnp.where(r == 0, 1.0, 0.0).astype(BF16)

    lp = lam_ref[...]
    lam_init = lp[4:5, 0:1]
    lam = (jnp.exp(jnp.sum(lp[0:1] * lp[1:2], axis=-1, keepdims=True))
           - jnp.exp(jnp.sum(lp[2:3] * lp[3:4], axis=-1, keepdims=True)) + lam_init)

    crow = crow_ref[0]
    posq = posq_ref[0]
    row = lax.broadcasted_iota(jnp.int32, (KPAD, TQ), 0)

    def tile(kj, rhs_pair, off, carry, diag):
        kt = kaug_ref[kj]
        vt = vaug_ref[kj]
        out = []
        for rhs, (mrun, acc) in zip(rhs_pair, carry):
            s = jnp.dot(kt, rhs, preferred_element_type=F32)
            if diag:
                s = s + dbias_ref[0]
            mt = jnp.max(s, axis=0, keepdims=True)
            if off is not None:
                mt = mt + off
            mnew = jnp.maximum(mrun, mt)
            alpha = jnp.exp2(mrun - mnew)
            sub = mnew if off is None else mnew - off
            p = jnp.exp2(s - sub).astype(BF16)
            acc = alpha * acc + jnp.dot(vt, p, preferred_element_type=F32)
            out.append((mnew, acc))
        return tuple(out)

    def qblock(qi, _):
        qt = qt_ref[0, qi]
        zero = jnp.zeros_like(qt)
        qm = [jnp.where((row >= ATT_QK * m) & (row < ATT_QK * (m + 1)), qt, zero)
              for m in range(2)]
        rhs_lo = tuple(q + posq for q in qm)
        rhs_up = tuple(q - posq for q in qm)
        init = tuple((jnp.full((1, TQ), -1e30, F32), jnp.zeros((V_ROWS, TQ), F32))
                     for _ in range(2))

        carry = tile(qi, tuple(qm), None, init, True)

        def lower(kj, c):
            off = crow * (-((qi - kj) * TK).astype(F32))
            return tile(kj, rhs_lo, off, c, False)

        def upper(kj, c):
            off = crow * (-((kj - qi) * TK).astype(F32))
            return tile(kj, rhs_up, off, c, False)

        carry = lax.fori_loop(0, qi, lower, carry)
        carry = lax.fori_loop(qi + 1, nk, upper, carry)

        (_, a1), (_, a2) = carry
        o1 = a1[:ATT_VDIM] / a1[ATT_VDIM:ATT_VDIM + 1]
        o2 = a2[:ATT_VDIM] / a2[ATT_VDIM:ATT_VDIM + 1]
        o = o1 - lam * o2
        ms = jnp.mean(o * o, axis=0, keepdims=True)
        o_ref[0, qi] = (o * lax.rsqrt(ms + EPS) * g_ref[...]).astype(BF16)
        return 0

    lax.fori_loop(0, nq, qblock, 0)


def _attention(lam_pack, gcol, consts, qt, k, vt):
    b, nq = qt.shape[0], qt.shape[1]
    nk = vt.shape[1]
    s = k.shape[1]
    crow, posk, posq, dbias = consts
    head = lambda shape: pl.BlockSpec((1,) + shape, lambda i, h: (h, 0, 0))
    return pl.pallas_call(
        _attn_kernel,
        out_shape=jax.ShapeDtypeStruct((b, nq, GROUP_WIDTH, TQ), BF16),
        grid=(b, ATT_HEADS),
        in_specs=[
            _full(lam_pack.shape), _full(gcol.shape),
            head((1, TQ)), head((TK, KPAD)), head((KPAD, TQ)), head((TK, TQ)),
            pl.BlockSpec((1, nq, KPAD, TQ), lambda i, h: (i, 0, h, 0)),
            pl.BlockSpec((1, s, KPAD), lambda i, h: (i, 0, h)),
            pl.BlockSpec((1, nk, ATT_VDIM, TK), lambda i, h: (i, 0, h, 0)),
        ],
        out_specs=pl.BlockSpec((1, nq, ATT_VDIM, TQ), lambda i, h: (i, 0, h, 0)),
        scratch_shapes=[pltpu.VMEM((nk, TK, KPAD), BF16), pltpu.VMEM((nk, V_ROWS, TK), BF16)],
        compiler_params=_params(("parallel", "parallel")),
        name="diff_attention",
    )(lam_pack, gcol, crow, posk, posq, dbias, qt, k, vt)


def _attention_consts():
    slopes = np.array([2.0 ** (-8.0 * (i + 1) / ATT_HEADS) for i in range(ATT_HEADS)], np.float64)
    c = (slopes * LOG2E).astype(np.float32)
    parts = []
    rem = c.astype(np.float32)
    for _ in range(N_SPLIT):
        p = rem.astype(BF16).astype(np.float32)
        parts.append(p)
        rem = (rem - p).astype(np.float32)
    rk = np.arange(TK, dtype=np.float32)
    rq = np.arange(TQ, dtype=np.float32)
    posk = np.zeros((ATT_HEADS, TK, KPAD), np.float32)
    posq = np.zeros((ATT_HEADS, KPAD, TQ), np.float32)
    for t in range(N_SPLIT):
        posk[:, :, POS_LANE + t] = rk[None, :]
        posk[:, :, POS_LANE + N_SPLIT + t] = parts[t][:, None]
        posq[:, POS_LANE + t, :] = parts[t][:, None]
        posq[:, POS_LANE + N_SPLIT + t, :] = -rq[None, :]
    dbias = -c[:, None, None] * np.abs(rq[None, None, :] - rk[None, :, None])
    crow = np.broadcast_to(c[:, None, None], (ATT_HEADS, 1, TQ))
    return (jnp.asarray(crow, F32), jnp.asarray(posk, BF16), jnp.asarray(posq, BF16),
            jnp.asarray(dbias, F32))


CONV_PAD = 16
CONV_ROWS = 128


def _split_dot(x, w):
    hi = x.astype(BF16)
    lo = (x - hi.astype(F32)).astype(BF16)
    return (jnp.dot(hi, w, preferred_element_type=F32)
            + jnp.dot(lo, w, preferred_element_type=F32))


def _conv_kernel(hb_ref, dww_ref, dwb_ref, lng_ref, lnb_ref, gavg_ref, pww_ref, pwb_ref,
                 o_ref, z_ref):
    s = hb_ref.shape[1]
    half = CONV_WIDTH // 2
    zeros = jnp.zeros((CONV_PAD, GROUP_WIDTH), F32)
    z_ref[0:CONV_PAD, :] = zeros
    z_ref[CONV_PAD + s:CONV_PAD + s + CONV_PAD, :] = zeros

    def glu(i, _):
        r0 = pl.multiple_of(i * CONV_ROWS, CONV_ROWS)
        h = hb_ref[0, pl.ds(r0, CONV_ROWS), :]
        z_ref[pl.ds(CONV_PAD + r0, CONV_ROWS), :] = h[:, :GROUP_WIDTH] * jax.nn.sigmoid(h[:, GROUP_WIDTH:])
        return 0

    lax.fori_loop(0, s // CONV_ROWS, glu, 0)

    gavg = gavg_ref[...]

    def chunk(i, _):
        r0 = pl.multiple_of(i * CONV_ROWS, CONV_ROWS)
        acc = jnp.zeros((CONV_ROWS, GROUP_WIDTH), F32) + dwb_ref[...]
        win = z_ref[pl.ds(r0, CONV_ROWS + 2 * CONV_PAD), :]
        span = CONV_ROWS + 2 * CONV_PAD - 8
        for j in range(8):
            wj = win[j:j + span]
            for t in range(CONV_WIDTH):
                off = CONV_PAD - half + t
                if off % 8 == j:
                    a = off - j
                    acc = acc + dww_ref[t:t + 1, :] * wj[a:a + CONV_ROWS]
        mu = _split_dot(acc, gavg)
        xc = acc - mu
        var = _split_dot(xc * xc, gavg)
        y = xc * lax.rsqrt(var + EPS) * lng_ref[...] + lnb_ref[...]
        y = y * jax.nn.sigmoid(y)
        out = jnp.dot(y.astype(BF16), pww_ref[...], preferred_element_type=F32) + pwb_ref[...]
        o_ref[0, pl.ds(r0, CONV_ROWS), :] = out.astype(BF16)
        return 0

    lax.fori_loop(0, s // CONV_ROWS, chunk, 0)


def _conformer_conv(hb, dww, dwb, lng, lnb, gavg, pww, pwb):
    b, s, _ = hb.shape
    return pl.pallas_call(
        _conv_kernel,
        out_shape=jax.ShapeDtypeStruct((b, s, GROUP_WIDTH), BF16),
        grid=(b,),
        in_specs=[pl.BlockSpec((1, s, 2 * GROUP_WIDTH), lambda i: (i, 0, 0)),
                  _full(dww.shape), _full(dwb.shape), _full(lng.shape), _full(lnb.shape),
                  _full(gavg.shape), _full(pww.shape), _full(pwb.shape)],
        out_specs=pl.BlockSpec((1, s, GROUP_WIDTH), lambda i: (i, 0, 0)),
        scratch_shapes=[pltpu.VMEM((s + 2 * CONV_PAD, GROUP_WIDTH), F32)],
        compiler_params=_params(("parallel",)),
        name="conformer_conv",
    )(hb, dww, dwb, lng, lnb, gavg, pww, pwb)


FNET_ROWS = 512


def _fnet_kernel(c_ref, cmat_ref, m1_ref, m3_ref, twc_ref, tws_ref, w_ref, b_ref,
                 o_ref, z_ref, bb_ref, y_ref):
    s = c_ref.shape[1]
    n1 = m1_ref.shape[0] // 2
    n2 = m3_ref.shape[0]
    gw = GROUP_WIDTH
    rows = min(FNET_ROWS, s)

    nl = gw // 128

    for i in range(s // rows):
        z = jnp.dot(c_ref[0, i * rows:(i + 1) * rows, :], cmat_ref[...],
                    preferred_element_type=F32)
        for p in range(2 * nl):
            z_ref[p, i * rows:(i + 1) * rows, :] = z[:, p * 128:(p + 1) * 128]

    m1 = m1_ref[...]
    for j in range(n2):
        zcat = jnp.concatenate(
            [jnp.concatenate([z_ref[comp * nl + p, pl.ds(j, n1, stride=n2), :] for p in range(nl)],
                             axis=1) for comp in range(2)], axis=0)
        a = jnp.dot(m1, zcat.astype(BF16), preferred_element_type=F32)
        tc = twc_ref[j]
        ts = tws_ref[j]
        for p in range(nl):
            ar = a[:n1, p * 128:(p + 1) * 128]
            ai = a[n1:, p * 128:(p + 1) * 128]
            bb_ref[p, pl.ds(j, n1, stride=2 * n2), :] = ar * tc + ai * ts
            bb_ref[p, pl.ds(n2 + j, n1, stride=2 * n2), :] = ai * tc - ar * ts

    m3 = m3_ref[...]
    for k1 in range(n1):
        rhs = jnp.concatenate([bb_ref[p, k1 * 2 * n2:(k1 + 1) * 2 * n2, :] for p in range(nl)],
                              axis=1).astype(BF16)
        y = jnp.dot(m3, rhs, preferred_element_type=F32)
        for p in range(nl):
            y_ref[p, pl.ds(k1, n2, stride=n1), :] = y[:, p * 128:(p + 1) * 128]

    norm = 1.0 / math.sqrt(s * (gw // FNET_GROUPS))
    for i in range(s // rows):
        y = jnp.concatenate([y_ref[p, i * rows:(i + 1) * rows, :] for p in range(nl)], axis=1)
        out = jnp.dot((y * norm).astype(BF16), w_ref[...], preferred_element_type=F32) + b_ref[...]
        o_ref[0, i * rows:(i + 1) * rows, :] = out.astype(BF16)


def _fnet_consts(s):
    n1 = 1 << (int(math.log2(s)) // 2)
    n2 = s // n1
    cw = GROUP_WIDTH // FNET_GROUPS
    ang = 2.0 * np.pi * np.outer(np.arange(cw), np.arange(cw)) / cw
    eye = np.eye(FNET_GROUPS)
    cmat = np.concatenate([np.kron(eye, np.cos(ang)), -np.kron(eye, np.sin(ang))], axis=1)
    a1 = 2.0 * np.pi * np.outer(np.arange(n1), np.arange(n1)) / n1
    m1 = np.block([[np.cos(a1), np.sin(a1)], [-np.sin(a1), np.cos(a1)]])
    a2 = 2.0 * np.pi * np.outer(np.arange(n2), np.arange(n2)) / n2
    m3 = np.concatenate([np.cos(a2), np.sin(a2)], axis=1)
    at = 2.0 * np.pi * np.outer(np.arange(n2), np.arange(n1)) / s
    twc = np.broadcast_to(np.cos(at)[:, :, None], (n2, n1, 128))
    tws = np.broadcast_to(np.sin(at)[:, :, None], (n2, n1, 128))
    return (jnp.asarray(cmat, BF16), jnp.asarray(m1, BF16), jnp.asarray(m3, BF16),
            jnp.asarray(twc, F32), jnp.asarray(tws, F32))


def _fourier_mix(c, consts, w_bd, bias):
    b, s, gw = c.shape
    cmat, m1, m3, twc, tws = consts
    return pl.pallas_call(
        _fnet_kernel,
        out_shape=jax.ShapeDtypeStruct((b, s, gw), BF16),
        grid=(b,),
        in_specs=[pl.BlockSpec((1, s, gw), lambda i: (i, 0, 0)),
                  _full(cmat.shape), _full(m1.shape), _full(m3.shape),
                  _full(twc.shape), _full(tws.shape), _full(w_bd.shape), _full(bias.shape)],
        out_specs=pl.BlockSpec((1, s, gw), lambda i: (i, 0, 0)),
        scratch_shapes=[pltpu.VMEM((2 * gw // 128, s, 128), F32),
                        pltpu.VMEM((gw // 128, 2 * s, 128), F32),
                        pltpu.VMEM((gw // 128, s, 128), F32)],
        compiler_params=_params(("parallel",)),
        name="fourier_mix",
    )(c, cmat, m1, m3, twc, tws, w_bd, bias)


def _sgu_kernel(u_ref, v_ref, g_ref, b_ref, w_ref, bias_ref, o_ref):
    v = v_ref[0]
    mu = jnp.mean(v, axis=-1, keepdims=True)
    xc = v - mu
    var = jnp.mean(xc * xc, axis=-1, keepdims=True)
    vn = (xc * lax.rsqrt(var + EPS) * g_ref[...] + b_ref[...]).astype(BF16)
    cg = GROUP_WIDTH // SGU_GROUPS
    lane = lax.broadcasted_iota(jnp.int32, (SGU_CHUNK, GROUP_WIDTH), 1)
    w = w_ref[...]
    for c in range(TM // SGU_CHUNK):
        rs = slice(c * SGU_CHUNK, (c + 1) * SGU_CHUNK)
        r = jnp.dot(w, vn[rs], preferred_element_type=F32)
        sv = r[(SGU_GROUPS - 1) * SGU_CHUNK:]
        for g in range(SGU_GROUPS - 2, -1, -1):
            sv = jnp.where(lane < (g + 1) * cg, r[g * SGU_CHUNK:(g + 1) * SGU_CHUNK], sv)
        o_ref[0, rs, :] = (u_ref[0, rs, :] * (sv + bias_ref[...])).astype(BF16)


def _spatial_gate(u, v, g, b, wcat, bias_plane):
    bsz, s, gw = u.shape
    tok = pl.BlockSpec((1, TM, gw), lambda i, j: (i, j, 0))
    return pl.pallas_call(
        _sgu_kernel,
        out_shape=jax.ShapeDtypeStruct((bsz, s, gw), BF16),
        grid=(bsz, s // TM),
        in_specs=[tok, tok, _full(g.shape), _full(b.shape), _full(wcat.shape),
                  _full(bias_plane.shape)],
        out_specs=tok,
        compiler_params=_params(("parallel", "parallel")),
        name="spatial_gate",
    )(u, v, g, b, wcat, bias_plane)


def _outproj_kernel(x_ref, ya_ref, yb_ref, yc_ref, yd_ref, w_ref, o_ref):
    acc = (x_ref[0]
           + jnp.dot(yb_ref[0], w_ref[1], preferred_element_type=F32)
           + jnp.dot(yc_ref[0], w_ref[2], preferred_element_type=F32)
           + jnp.dot(yd_ref[0], w_ref[3], preferred_element_type=F32))
    for j in range(TM // TQ):
        ya = lax.dot_general(ya_ref[0, j], w_ref[0], (((0,), (0,)), ((), ())),
                             preferred_element_type=F32)
        o_ref[0, j * TQ:(j + 1) * TQ, :] = acc[j * TQ:(j + 1) * TQ] + ya


def _outproj(x, ya_t, yb, yc, yd, w):
    b, s, d = x.shape
    gw = GROUP_WIDTH
    tok = lambda width: pl.BlockSpec((1, TM, width), lambda i, j: (i, j, 0))
    return pl.pallas_call(
        _outproj_kernel,
        out_shape=jax.ShapeDtypeStruct((b, s, d), F32),
        grid=(b, s // TM),
        in_specs=[tok(d), pl.BlockSpec((1, TM // TQ, gw, TQ), lambda i, j: (i, j, 0, 0)),
                  tok(gw), tok(gw), tok(gw), _full(w.shape)],
        out_specs=tok(d),
        compiler_params=_params(("parallel", "parallel")),
        name="outproj",
    )(x, ya_t, yb, yc, yd, w)


FF_CHUNK = 1024


def _mlp_kernel(x_ref, g_ref, wu_ref, wd_ref, o_ref):
    x = x_ref[0]
    ms = jnp.mean(x * x, axis=-1, keepdims=True)
    xn = (x * lax.rsqrt(ms + EPS) * g_ref[...]).astype(BF16)
    acc = x
    for c in range(D_FF // FF_CHUNK):
        cs = slice(c * FF_CHUNK, (c + 1) * FF_CHUNK)
        h = jnp.maximum(jnp.dot(xn, wu_ref[:, cs], preferred_element_type=F32), 0.0)
        acc = acc + jnp.dot((h * h).astype(BF16), wd_ref[cs, :], preferred_element_type=F32)
    o_ref[0] = acc


def _mlp(x, g, wu, wd):
    b, s, d = x.shape
    tok = pl.BlockSpec((1, TM, d), lambda i, j: (i, j, 0))
    const = lambda shape: pl.BlockSpec(shape, lambda i, j: (0, 0), pipeline_mode=pl.Buffered(1))
    return pl.pallas_call(
        _mlp_kernel,
        out_shape=jax.ShapeDtypeStruct((b, s, d), F32),
        grid=(b, s // TM),
        in_specs=[tok, _full(g.shape), const(wu.shape), const(wd.shape)],
        out_specs=tok,
        compiler_params=_params(("parallel", "parallel")),
        name="mlp",
    )(x, g, wu, wd)


def _final_norm_kernel(x_ref, g_ref, o_ref):
    x = x_ref[0]
    ms = jnp.mean(x * x, axis=-1, keepdims=True)
    o_ref[0] = x * lax.rsqrt(ms + EPS) * g_ref[...]


def _final_norm(x, g):
    b, s, d = x.shape
    tok = pl.BlockSpec((1, TM, d), lambda i, j: (i, j, 0))
    return pl.pallas_call(
        _final_norm_kernel,
        out_shape=jax.ShapeDtypeStruct((b, s, d), F32),
        grid=(b, s // TM),
        in_specs=[tok, _full(g.shape)],
        out_specs=tok,
        compiler_params=_params(("parallel", "parallel")),
        name="final_norm",
    )(x, g)


def _block_diag(blocks):
    g, r, c = blocks.shape
    eye = jnp.eye(g, dtype=blocks.dtype)
    return (eye[:, None, :, None] * blocks[:, :, None, :]).reshape(g * r, g * c)


def _layer_weights(w_in_l):
    gw = GROUP_WIDTH
    d = w_in_l.shape[0]
    pad = lambda w: jnp.pad(w.reshape(d, ATT_HEADS, 2 * ATT_QK),
                            ((0, 0), (0, 0), (0, KPAD - 2 * ATT_QK))).reshape(d, ATT_HEADS * KPAD)
    wq, wk, wv = w_in_l[:, :gw], w_in_l[:, gw:2 * gw], w_in_l[:, 2 * gw:3 * gw]
    wn = jnp.concatenate([pad(wk), w_in_l[:, 3 * gw:]], axis=1).astype(BF16)
    wt = jnp.concatenate([pad(wq), wv], axis=1).T.astype(BF16)
    return wn, wt


def kernel(x, norm1_g, w_in, lam_q1, lam_k1, lam_q2, lam_k2, subln_g, conv_dw_w, conv_dw_b, conv_ln_g, conv_ln_b, conv_pw_w, conv_pw_b, fnet_w, fnet_b, sgu_ln_g, sgu_ln_b, sgu_w, sgu_b, w_out, norm2_g, w_up, w_down, final_g):
    bsz, seq, d = x.shape
    depth = w_in.shape[0]
    gw = GROUP_WIDTH
    assert seq % TM == 0 and d == D_MODEL

    attn_consts = _attention_consts()
    fnet_consts = _fnet_consts(seq)
    cg = gw // CONV_GROUPS
    gavg = jnp.asarray(np.kron(np.eye(CONV_GROUPS), np.full((cg, cg), 1.0 / cg)), BF16)
    row = lambda v: v.reshape(1, -1).astype(F32)

    for l in range(depth):
        wn, wt = _layer_weights(w_in[l])
        k, hb, hc, u, v, qt, vt = _inproj(x, row(norm1_g[l]), wn, wt)

        lam_init = 0.8 - 0.6 * math.exp(-0.3 * l)
        lam_pack = jnp.concatenate(
            [lam_q1[l][None], lam_k1[l][None], lam_q2[l][None], lam_k2[l][None],
             jnp.full((4, ATT_QK), lam_init, F32)], axis=0).astype(F32)
        gcol = (subln_g[l].astype(F32) * (1.0 - lam_init)).reshape(ATT_VDIM, 1)
        ya_t = _attention(lam_pack, gcol, attn_consts, qt, k, vt)

        yb = _conformer_conv(hb, conv_dw_w[l].astype(F32), row(conv_dw_b[l]), row(conv_ln_g[l]),
                             row(conv_ln_b[l]), gavg, conv_pw_w[l].astype(BF16), row(conv_pw_b[l]))

        yc = _fourier_mix(hc, fnet_consts, _block_diag(fnet_w[l]).astype(BF16), row(fnet_b[l]))

        wcat = sgu_w[l].reshape(SGU_GROUPS * SGU_CHUNK, SGU_CHUNK).astype(BF16)
        bias_plane = jnp.repeat(sgu_b[l].T.astype(F32), gw // SGU_GROUPS, axis=1)
        yd = _spatial_gate(u, v, row(sgu_ln_g[l]), row(sgu_ln_b[l]), wcat, bias_plane)

        x = _outproj(x, ya_t, yb, yc, yd, w_out[l].reshape(4, gw, d).astype(BF16))
        x = _mlp(x, row(norm2_g[l]), w_up[l].astype(BF16), w_down[l].astype(BF16))

    return _final_norm(x, row(final_g))
```

```python
import functools
import math

import numpy as np
import jax
import jax.numpy as jnp
from jax import lax
from jax.experimental import pallas as pl
from jax.experimental.pallas import tpu as pltpu

D_MODEL = 1024
GROUP_WIDTH = 256
ATT_HEADS = 4
ATT_VDIM = 64
ATT_QK = 32
CONV_WIDTH = 31
CONV_GROUPS = 4
FNET_GROUPS = 4
SGU_CHUNK = 128
SGU_GROUPS = 4
D_FF = 4 * D_MODEL
EPS = 1e-6
LOG2E = 1.4426950408889634

KPAD = 128
POS_LANE = 2 * ATT_QK
N_SPLIT = 3
TQ = 256
TK = 256
V_ROWS = ATT_VDIM + 16
Q_STREAMS = 2
TM = 512
VMEM_LIMIT = 56 * 1024 * 1024

BF16 = jnp.bfloat16
F32 = jnp.float32


def _params(sem, vmem=VMEM_LIMIT):
    return pltpu.CompilerParams(dimension_semantics=sem, vmem_limit_bytes=vmem)


def _full(shape):
    n = len(shape)
    return pl.BlockSpec(shape, lambda *_: (0,) * n)


def _inproj_kernel(x_ref, g_ref, wn_ref, wt_ref,
                   k_ref, hb_ref, hc_ref, u_ref, v_ref, qt_ref, vt_ref):
    x = x_ref[0]
    ms = jnp.mean(x * x, axis=-1, keepdims=True)
    xn = (x * lax.rsqrt(ms + EPS) * g_ref[...]).astype(BF16)
    nat = jnp.dot(xn, wn_ref[...], preferred_element_type=F32)
    tr = lax.dot_general(wt_ref[...], xn, (((1,), (1,)), ((), ())),
                         preferred_element_type=F32)
    kw = ATT_HEADS * KPAD
    k_ref[0] = nat[:, :kw].astype(BF16)
    hb_ref[0] = nat[:, kw:kw + 2 * GROUP_WIDTH]
    o = kw + 2 * GROUP_WIDTH
    hc_ref[0] = nat[:, o:o + GROUP_WIDTH].astype(BF16)
    u_ref[0] = nat[:, o + GROUP_WIDTH:o + 2 * GROUP_WIDTH]
    v_ref[0] = nat[:, o + 2 * GROUP_WIDTH:o + 3 * GROUP_WIDTH]
    qscale = (ATT_QK ** -0.5) * LOG2E
    for j in range(TM // TQ):
        qt_ref[0, j] = (tr[:kw, j * TQ:(j + 1) * TQ] * qscale).astype(BF16)
    for j in range(TM // TK):
        vt_ref[0, j] = tr[kw:, j * TK:(j + 1) * TK].astype(BF16)


def _inproj(x, g, wn, wt):
    b, s, d = x.shape
    kw = ATT_HEADS * KPAD
    nn = wn.shape[1]
    nt = wt.shape[0]
    grid = (b, s // TM)
    tok = lambda w: pl.BlockSpec((1, TM, w), lambda i, j: (i, j, 0))
    out_shape = (
        jax.ShapeDtypeStruct((b, s, kw), BF16),
        jax.ShapeDtypeStruct((b, s, 2 * GROUP_WIDTH), F32),
        jax.ShapeDtypeStruct((b, s, GROUP_WIDTH), BF16),
        jax.ShapeDtypeStruct((b, s, GROUP_WIDTH), F32),
        jax.ShapeDtypeStruct((b, s, GROUP_WIDTH), F32),
        jax.ShapeDtypeStruct((b, s // TQ, kw, TQ), BF16),
        jax.ShapeDtypeStruct((b, s // TK, GROUP_WIDTH, TK), BF16),
    )
    out_specs = (
        tok(kw), tok(2 * GROUP_WIDTH), tok(GROUP_WIDTH), tok(GROUP_WIDTH), tok(GROUP_WIDTH),
        pl.BlockSpec((1, TM // TQ, kw, TQ), lambda i, j: (i, j, 0, 0)),
        pl.BlockSpec((1, TM // TK, GROUP_WIDTH, TK), lambda i, j: (i, j, 0, 0)),
    )
    return pl.pallas_call(
        _inproj_kernel,
        out_shape=out_shape,
        grid=grid,
        in_specs=[tok(d), _full((1, d)), _full((d, nn)), _full((nt, d))],
        out_specs=out_specs,
        compiler_params=_params(("parallel", "parallel")),
        name="inproj",
    )(x, g, wn, wt)


def _attn_kernel(lam_ref, g_ref, crow_ref, posk_ref, posq_ref, dbias_ref,
                 qt_ref, k_ref, vt_ref, o_ref, kaug_ref, vaug_ref, rhs_ref, sa_ref, sb_ref):
    nq = qt_ref.shape[1]
    nk = vt_ref.shape[1]
    ns = Q_STREAMS

    for j in range(nk):
        kaug_ref[j] = k_ref[0, j * TK:(j + 1) * TK, :] + posk_ref[0]
        vaug_ref[j, :ATT_VDIM, :] = vt_ref[0, j]
        r = lax.broadcasted_iota(jnp.int32, (V_ROWS - ATT_VDIM, TK), 0)
        vaug_ref[j, ATT_VDIM:, :] = jnp.where(r == 0, 1.0, 0.0).astype(BF16)

    lp = lam_ref[...]
    lam_init = lp[4:5, 0:1]
    lam = (jnp.exp(jnp.sum(lp[0:1] * lp[1:2], axis=-1, keepdims=True))
           - jnp.exp(jnp.sum(lp[2:3] * lp[3:4], axis=-1, keepdims=True)) + lam_init)

    crow = crow_ref[0]
    posq = posq_ref[0]
    row = lax.broadcasted_iota(jnp.int32, (KPAD, TQ), 0)

    def key_block(u, qi):
        return jnp.where(u == 0, qi, u - 1 + (u - 1 >= qi).astype(jnp.int32))

    def scores(buf, g, m, kj, rhs, diag):
        s = jnp.dot(kaug_ref[kj], rhs, preferred_element_type=F32)
        if diag:
            s = s + dbias_ref[0]
        buf[2 * g + m] = s
        return jnp.max(s, axis=0, keepdims=True)

    def consume(buf, g, m, kj, qi, mt, mrun, acc):
        off = crow * (-(jnp.abs(kj - qi) * TK).astype(F32))
        mnew = jnp.maximum(mrun, mt + off)
        alpha = jnp.exp2(mrun - mnew)
        p = jnp.exp2(buf[2 * g + m] - (mnew - off)).astype(BF16)
        acc = alpha * acc + jnp.dot(vaug_ref[kj], p, preferred_element_type=F32)
        return mnew, acc

    def qgroup(qg, _):
        qis = [qg * ns + g for g in range(ns)]
        mts = []
        for g, qi in enumerate(qis):
            qt = qt_ref[0, qi]
            zero = jnp.zeros_like(qt)
            for m in range(2):
                qm = jnp.where((row >= ATT_QK * m) & (row < ATT_QK * (m + 1)), qt, zero)
                rhs_ref[(2 * g + 0) * 2 + m] = qm + posq
                rhs_ref[(2 * g + 1) * 2 + m] = qm - posq
                mts.append(scores(sa_ref, g, m, qi, qm, True))

        def produce(buf, u):
            out = []
            for g, qi in enumerate(qis):
                kj = key_block(u, qi)
                side = (kj > qi).astype(jnp.int32)
                for m in range(2):
                    out.append(scores(buf, g, m, kj, rhs_ref[(2 * g + side) * 2 + m], False))
            return tuple(out)

        def finish(buf, u, mts, mruns, accs):
            new_mruns, new_accs = [], []
            for g, qi in enumerate(qis):
                kp = key_block(u, qi)
                for m in range(2):
                    i = 2 * g + m
                    mr, ac = consume(buf, g, m, kp, qi, mts[i], mruns[i], accs[i])
                    new_mruns.append(mr)
                    new_accs.append(ac)
            return tuple(new_mruns), tuple(new_accs)

        def pair(t, carry):
            mts, mruns, accs = carry
            u = 2 * t + 1
            mts_b = produce(sb_ref, u)
            mruns, accs = finish(sa_ref, u - 1, mts, mruns, accs)
            mts_a = produce(sa_ref, u + 1)
            mruns, accs = finish(sb_ref, u, mts_b, mruns, accs)
            return mts_a, mruns, accs

        init = (tuple(mts),
                tuple(jnp.full((1, TQ), -1e30, F32) for _ in range(2 * ns)),
                tuple(jnp.zeros((V_ROWS, TQ), F32) for _ in range(2 * ns)))
        mts, mruns, accs = lax.fori_loop(0, (nk - 2) // 2, pair, init)
        mts_b = produce(sb_ref, nk - 1)
        mruns, accs = finish(sa_ref, nk - 2, mts, mruns, accs)
        mruns, accs = finish(sb_ref, nk - 1, mts_b, mruns, accs)
        for g, qi in enumerate(qis):
            outs = [accs[2 * g + m][:ATT_VDIM] / accs[2 * g + m][ATT_VDIM:ATT_VDIM + 1]
                    for m in range(2)]
            o = outs[0] - lam * outs[1]
            ms = jnp.mean(o * o, axis=0, keepdims=True)
            o_ref[0, qi] = (o * lax.rsqrt(ms + EPS) * g_ref[...]).astype(BF16)
        return 0

    lax.fori_loop(0, nq // ns, qgroup, 0)


def _attention(lam_pack, gcol, consts, qt, k, vt):
    b, nq = qt.shape[0], qt.shape[1]
    nk = vt.shape[1]
    s = k.shape[1]
    crow, posk, posq, dbias = consts
    head = lambda shape: pl.BlockSpec((1,) + shape, lambda i, h: (h, 0, 0))
    return pl.pallas_call(
        _attn_kernel,
        out_shape=jax.ShapeDtypeStruct((b, nq, GROUP_WIDTH, TQ), BF16),
        grid=(b, ATT_HEADS),
        in_specs=[
            _full(lam_pack.shape), _full(gcol.shape),
            head((1, TQ)), head((TK, KPAD)), head((KPAD, TQ)), head((TK, TQ)),
            pl.BlockSpec((1, nq, KPAD, TQ), lambda i, h: (i, 0, h, 0)),
            pl.BlockSpec((1, s, KPAD), lambda i, h: (i, 0, h)),
            pl.BlockSpec((1, nk, ATT_VDIM, TK), lambda i, h: (i, 0, h, 0)),
        ],
        out_specs=pl.BlockSpec((1, nq, ATT_VDIM, TQ), lambda i, h: (i, 0, h, 0)),
        scratch_shapes=[pltpu.VMEM((nk, TK, KPAD), BF16), pltpu.VMEM((nk, V_ROWS, TK), BF16),
                        pltpu.VMEM((4 * Q_STREAMS, KPAD, TQ), BF16),
                        pltpu.VMEM((2 * Q_STREAMS, TK, TQ), F32),
                        pltpu.VMEM((2 * Q_STREAMS, TK, TQ), F32)],
        compiler_params=_params(("parallel", "parallel")),
        name="diff_attention",
    )(lam_pack, gcol, crow, posk, posq, dbias, qt, k, vt)


def _attention_consts():
    slopes = np.array([2.0 ** (-8.0 * (i + 1) / ATT_HEADS) for i in range(ATT_HEADS)], np.float64)
    c = (slopes * LOG2E).astype(np.float32)
    parts = []
    rem = c.astype(np.float32)
    for _ in range(N_SPLIT):
        p = rem.astype(BF16).astype(np.float32)
        parts.append(p)
        rem = (rem - p).astype(np.float32)
    rk = np.arange(TK, dtype=np.float32)
    rq = np.arange(TQ, dtype=np.float32)
    posk = np.zeros((ATT_HEADS, TK, KPAD), np.float32)
    posq = np.zeros((ATT_HEADS, KPAD, TQ), np.float32)
    for t in range(N_SPLIT):
        posk[:, :, POS_LANE + t] = rk[None, :]
        posk[:, :, POS_LANE + N_SPLIT + t] = parts[t][:, None]
        posq[:, POS_LANE + t, :] = parts[t][:, None]
        posq[:, POS_LANE + N_SPLIT + t, :] = -rq[None, :]
    dbias = -c[:, None, None] * np.abs(rq[None, None, :] - rk[None, :, None])
    crow = np.broadcast_to(c[:, None, None], (ATT_HEADS, 1, TQ))
    return (jnp.asarray(crow, F32), jnp.asarray(posk, BF16), jnp.asarray(posq, BF16),
            jnp.asarray(dbias, F32))


CONV_PAD = 16
CONV_ROWS = 128


def _split_dot(x, w):
    hi = x.astype(BF16)
    lo = (x - hi.astype(F32)).astype(BF16)
    return (jnp.dot(hi, w, preferred_element_type=F32)
            + jnp.dot(lo, w, preferred_element_type=F32))


def _conv_kernel(hb_ref, dww_ref, dwb_ref, lng_ref, lnb_ref, gavg_ref, pww_ref, pwb_ref,
                 o_ref, z_ref):
    s = hb_ref.shape[1]
    half = CONV_WIDTH // 2
    zeros = jnp.zeros((CONV_PAD, GROUP_WIDTH), F32)
    z_ref[0:CONV_PAD, :] = zeros
    z_ref[CONV_PAD + s:CONV_PAD + s + CONV_PAD, :] = zeros

    def glu(i, _):
        r0 = pl.multiple_of(i * CONV_ROWS, CONV_ROWS)
        h = hb_ref[0, pl.ds(r0, CONV_ROWS), :]
        z_ref[pl.ds(CONV_PAD + r0, CONV_ROWS), :] = h[:, :GROUP_WIDTH] * jax.nn.sigmoid(h[:, GROUP_WIDTH:])
        return 0

    lax.fori_loop(0, s // CONV_ROWS, glu, 0)

    gavg = gavg_ref[...]

    def chunk(i, _):
        r0 = pl.multiple_of(i * CONV_ROWS, CONV_ROWS)
        acc = jnp.zeros((CONV_ROWS, GROUP_WIDTH), F32) + dwb_ref[...]
        win = z_ref[pl.ds(r0, CONV_ROWS + 2 * CONV_PAD), :]
        span = CONV_ROWS + 2 * CONV_PAD - 8
        for j in range(8):
            wj = win[j:j + span]
            for t in range(CONV_WIDTH):
                off = CONV_PAD - half + t
                if off % 8 == j:
                    a = off - j
                    acc = acc + dww_ref[t:t + 1, :] * wj[a:a + CONV_ROWS]
        mu = _split_dot(acc, gavg)
        xc = acc - mu
        var = _split_dot(xc * xc, gavg)
        y = xc * lax.rsqrt(var + EPS) * lng_ref[...] + lnb_ref[...]
        y = y * jax.nn.sigmoid(y)
        out = jnp.dot(y.astype(BF16), pww_ref[...], preferred_element_type=F32) + pwb_ref[...]
        o_ref[0, pl.ds(r0, CONV_ROWS), :] = out.astype(BF16)
        return 0

    lax.fori_loop(0, s // CONV_ROWS, chunk, 0)


def _conformer_conv(hb, dww, dwb, lng, lnb, gavg, pww, pwb):
    b, s, _ = hb.shape
    return pl.pallas_call(
        _conv_kernel,
        out_shape=jax.ShapeDtypeStruct((b, s, GROUP_WIDTH), BF16),
        grid=(b,),
        in_specs=[pl.BlockSpec((1, s, 2 * GROUP_WIDTH), lambda i: (i, 0, 0)),
                  _full(dww.shape), _full(dwb.shape), _full(lng.shape), _full(lnb.shape),
                  _full(gavg.shape), _full(pww.shape), _full(pwb.shape)],
        out_specs=pl.BlockSpec((1, s, GROUP_WIDTH), lambda i: (i, 0, 0)),
        scratch_shapes=[pltpu.VMEM((s + 2 * CONV_PAD, GROUP_WIDTH), F32)],
        compiler_params=_params(("parallel",)),
        name="conformer_conv",
    )(hb, dww, dwb, lng, lnb, gavg, pww, pwb)


FNET_ROWS = 512


def _fnet_kernel(c_ref, cmat_ref, m1_ref, m3_ref, twc_ref, tws_ref, w_ref, b_ref,
                 o_ref, z_ref, bb_ref, y_ref):
    s = c_ref.shape[1]
    n1 = m1_ref.shape[0] // 2
    n2 = m3_ref.shape[0]
    gw = GROUP_WIDTH
    rows = min(FNET_ROWS, s)

    nl = gw // 128

    for i in range(s // rows):
        z = jnp.dot(c_ref[0, i * rows:(i + 1) * rows, :], cmat_ref[...],
                    preferred_element_type=F32)
        for p in range(2 * nl):
            z_ref[p, i * rows:(i + 1) * rows, :] = z[:, p * 128:(p + 1) * 128]

    m1 = m1_ref[...]
    for j in range(n2):
        zcat = jnp.concatenate(
            [jnp.concatenate([z_ref[comp * nl + p, pl.ds(j, n1, stride=n2), :] for p in range(nl)],
                             axis=1) for comp in range(2)], axis=0)
        a = jnp.dot(m1, zcat.astype(BF16), preferred_element_type=F32)
        tc = twc_ref[j]
        ts = tws_ref[j]
        for p in range(nl):
            ar = a[:n1, p * 128:(p + 1) * 128]
            ai = a[n1:, p * 128:(p + 1) * 128]
            bb_ref[p, pl.ds(j, n1, stride=2 * n2), :] = ar * tc + ai * ts
            bb_ref[p, pl.ds(n2 + j, n1, stride=2 * n2), :] = ai * tc - ar * ts

    m3 = m3_ref[...]
    for k1 in range(n1):
        rhs = jnp.concatenate([bb_ref[p, k1 * 2 * n2:(k1 + 1) * 2 * n2, :] for p in range(nl)],
                              axis=1).astype(BF16)
        y = jnp.dot(m3, rhs, preferred_element_type=F32)
        for p in range(nl):
            y_ref[p, pl.ds(k1, n2, stride=n1), :] = y[:, p * 128:(p + 1) * 128]

    norm = 1.0 / math.sqrt(s * (gw // FNET_GROUPS))
    for i in range(s // rows):
        y = jnp.concatenate([y_ref[p, i * rows:(i + 1) * rows, :] for p in range(nl)], axis=1)
        out = jnp.dot((y * norm).astype(BF16), w_ref[...], preferred_element_type=F32) + b_ref[...]
        o_ref[0, i * rows:(i + 1) * rows, :] = out.astype(BF16)


def _fnet_consts(s):
    n1 = 1 << (int(math.log2(s)) // 2)
    n2 = s // n1
    cw = GROUP_WIDTH // FNET_GROUPS
    ang = 2.0 * np.pi * np.outer(np.arange(cw), np.arange(cw)) / cw
    eye = np.eye(FNET_GROUPS)
    cmat = np.concatenate([np.kron(eye, np.cos(ang)), -np.kron(eye, np.sin(ang))], axis=1)
    a1 = 2.0 * np.pi * np.outer(np.arange(n1), np.arange(n1)) / n1
    m1 = np.block([[np.cos(a1), np.sin(a1)], [-np.sin(a1), np.cos(a1)]])
    a2 = 2.0 * np.pi * np.outer(np.arange(n2), np.arange(n2)) / n2
    m3 = np.concatenate([np.cos(a2), np.sin(a2)], axis=1)
    at = 2.0 * np.pi * np.outer(np.arange(n2), np.arange(n1)) / s
    twc = np.broadcast_to(np.cos(at)[:, :, None], (n2, n1, 128))
    tws = np.broadcast_to(np.sin(at)[:, :, None], (n2, n1, 128))
    return (jnp.asarray(cmat, BF16), jnp.asarray(m1, BF16), jnp.asarray(m3, BF16),
            jnp.asarray(twc, F32), jnp.asarray(tws, F32))


def _fourier_mix(c, consts, w_bd, bias):
    b, s, gw = c.shape
    cmat, m1, m3, twc, tws = consts
    return pl.pallas_call(
        _fnet_kernel,
        out_shape=jax.ShapeDtypeStruct((b, s, gw), BF16),
        grid=(b,),
        in_specs=[pl.BlockSpec((1, s, gw), lambda i: (i, 0, 0)),
                  _full(cmat.shape), _full(m1.shape), _full(m3.shape),
                  _full(twc.shape), _full(tws.shape), _full(w_bd.shape), _full(bias.shape)],
        out_specs=pl.BlockSpec((1, s, gw), lambda i: (i, 0, 0)),
        scratch_shapes=[pltpu.VMEM((2 * gw // 128, s, 128), F32),
                        pltpu.VMEM((gw // 128, 2 * s, 128), F32),
                        pltpu.VMEM((gw // 128, s, 128), F32)],
        compiler_params=_params(("parallel",)),
        name="fourier_mix",
    )(c, cmat, m1, m3, twc, tws, w_bd, bias)


def _sgu_kernel(u_ref, v_ref, g_ref, b_ref, w_ref, bias_ref, o_ref):
    v = v_ref[0]
    mu = jnp.mean(v, axis=-1, keepdims=True)
    xc = v - mu
    var = jnp.mean(xc * xc, axis=-1, keepdims=True)
    vn = (xc * lax.rsqrt(var + EPS) * g_ref[...] + b_ref[...]).astype(BF16)
    cg = GROUP_WIDTH // SGU_GROUPS
    lane = lax.broadcasted_iota(jnp.int32, (SGU_CHUNK, GROUP_WIDTH), 1)
    w = w_ref[...]
    for c in range(TM // SGU_CHUNK):
        rs = slice(c * SGU_CHUNK, (c + 1) * SGU_CHUNK)
        r = jnp.dot(w, vn[rs], preferred_element_type=F32)
        sv = r[(SGU_GROUPS - 1) * SGU_CHUNK:]
        for g in range(SGU_GROUPS - 2, -1, -1):
            sv = jnp.where(lane < (g + 1) * cg, r[g * SGU_CHUNK:(g + 1) * SGU_CHUNK], sv)
        o_ref[0, rs, :] = (u_ref[0, rs, :] * (sv + bias_ref[...])).astype(BF16)


def _spatial_gate(u, v, g, b, wcat, bias_plane):
    bsz, s, gw = u.shape
    tok = pl.BlockSpec((1, TM, gw), lambda i, j: (i, j, 0))
    return pl.pallas_call(
        _sgu_kernel,
        out_shape=jax.ShapeDtypeStruct((bsz, s, gw), BF16),
        grid=(bsz, s // TM),
        in_specs=[tok, tok, _full(g.shape), _full(b.shape), _full(wcat.shape),
                  _full(bias_plane.shape)],
        out_specs=tok,
        compiler_params=_params(("parallel", "parallel")),
        name="spatial_gate",
    )(u, v, g, b, wcat, bias_plane)


def _outproj_kernel(x_ref, ya_ref, yb_ref, yc_ref, yd_ref, w_ref, o_ref):
    acc = (x_ref[0]
           + jnp.dot(yb_ref[0], w_ref[1], preferred_element_type=F32)
           + jnp.dot(yc_ref[0], w_ref[2], preferred_element_type=F32)
           + jnp.dot(yd_ref[0], w_ref[3], preferred_element_type=F32))
    for j in range(TM // TQ):
        ya = lax.dot_general(ya_ref[0, j], w_ref[0], (((0,), (0,)), ((), ())),
                             preferred_element_type=F32)
        o_ref[0, j * TQ:(j + 1) * TQ, :] = acc[j * TQ:(j + 1) * TQ] + ya


def _outproj(x, ya_t, yb, yc, yd, w):
    b, s, d = x.shape
    gw = GROUP_WIDTH
    tok = lambda width: pl.BlockSpec((1, TM, width), lambda i, j: (i, j, 0))
    return pl.pallas_call(
        _outproj_kernel,
        out_shape=jax.ShapeDtypeStruct((b, s, d), F32),
        grid=(b, s // TM),
        in_specs=[tok(d), pl.BlockSpec((1, TM // TQ, gw, TQ), lambda i, j: (i, j, 0, 0)),
                  tok(gw), tok(gw), tok(gw), _full(w.shape)],
        out_specs=tok(d),
        compiler_params=_params(("parallel", "parallel")),
        name="outproj",
    )(x, ya_t, yb, yc, yd, w)


FF_CHUNK = 1024


def _mlp_kernel(x_ref, g_ref, wu_ref, wd_ref, o_ref):
    x = x_ref[0]
    ms = jnp.mean(x * x, axis=-1, keepdims=True)
    xn = (x * lax.rsqrt(ms + EPS) * g_ref[...]).astype(BF16)
    acc = x
    for c in range(D_FF // FF_CHUNK):
        cs = slice(c * FF_CHUNK, (c + 1) * FF_CHUNK)
        h = jnp.maximum(jnp.dot(xn, wu_ref[:, cs], preferred_element_type=F32), 0.0)
        acc = acc + jnp.dot((h * h).astype(BF16), wd_ref[cs, :], preferred_element_type=F32)
    o_ref[0] = acc


def _mlp(x, g, wu, wd):
    b, s, d = x.shape
    tok = pl.BlockSpec((1, TM, d), lambda i, j: (i, j, 0))
    const = lambda shape: pl.BlockSpec(shape, lambda i, j: (0, 0), pipeline_mode=pl.Buffered(1))
    return pl.pallas_call(
        _mlp_kernel,
        out_shape=jax.ShapeDtypeStruct((b, s, d), F32),
        grid=(b, s // TM),
        in_specs=[tok, _full(g.shape), const(wu.shape), const(wd.shape)],
        out_specs=tok,
        compiler_params=_params(("parallel", "parallel")),
        name="mlp",
    )(x, g, wu, wd)


def _final_norm_kernel(x_ref, g_ref, o_ref):
    x = x_ref[0]
    ms = jnp.mean(x * x, axis=-1, keepdims=True)
    o_ref[0] = x * lax.rsqrt(ms + EPS) * g_ref[...]


def _final_norm(x, g):
    b, s, d = x.shape
    tok = pl.BlockSpec((1, TM, d), lambda i, j: (i, j, 0))
    return pl.pallas_call(
        _final_norm_kernel,
        out_shape=jax.ShapeDtypeStruct((b, s, d), F32),
        grid=(b, s // TM),
        in_specs=[tok, _full(g.shape)],
        out_specs=tok,
        compiler_params=_params(("parallel", "parallel")),
        name="final_norm",
    )(x, g)


def _block_diag(blocks):
    g, r, c = blocks.shape
    eye = jnp.eye(g, dtype=blocks.dtype)
    return (eye[:, None, :, None] * blocks[:, :, None, :]).reshape(g * r, g * c)


def _layer_weights(w_in_l):
    gw = GROUP_WIDTH
    d = w_in_l.shape[0]
    pad = lambda w: jnp.pad(w.reshape(d, ATT_HEADS, 2 * ATT_QK),
                            ((0, 0), (0, 0), (0, KPAD - 2 * ATT_QK))).reshape(d, ATT_HEADS * KPAD)
    wq, wk, wv = w_in_l[:, :gw], w_in_l[:, gw:2 * gw], w_in_l[:, 2 * gw:3 * gw]
    wn = jnp.concatenate([pad(wk), w_in_l[:, 3 * gw:]], axis=1).astype(BF16)
    wt = jnp.concatenate([pad(wq), wv], axis=1).T.astype(BF16)
    return wn, wt


def kernel(x, norm1_g, w_in, lam_q1, lam_k1, lam_q2, lam_k2, subln_g, conv_dw_w, conv_dw_b, conv_ln_g, conv_ln_b, conv_pw_w, conv_pw_b, fnet_w, fnet_b, sgu_ln_g, sgu_ln_b, sgu_w, sgu_b, w_out, norm2_g, w_up, w_down, final_g):
    bsz, seq, d = x.shape
    depth = w_in.shape[0]
    gw = GROUP_WIDTH
    assert seq % TM == 0 and d == D_MODEL

    attn_consts = _attention_consts()
    fnet_consts = _fnet_consts(seq)
    cg = gw // CONV_GROUPS
    gavg = jnp.asarray(np.kron(np.eye(CONV_GROUPS), np.full((cg, cg), 1.0 / cg)), BF16)
    row = lambda v: v.reshape(1, -1).astype(F32)

    for l in range(depth):
        wn, wt = _layer_weights(w_in[l])
        k, hb, hc, u, v, qt, vt = _inproj(x, row(norm1_g[l]), wn, wt)

        lam_init = 0.8 - 0.6 * math.exp(-0.3 * l)
        lam_pack = jnp.concatenate(
            [lam_q1[l][None], lam_k1[l][None], lam_q2[l][None], lam_k2[l][None],
             jnp.full((4, ATT_QK), lam_init, F32)], axis=0).astype(F32)
        gcol = (subln_g[l].astype(F32) * (1.0 - lam_init)).reshape(ATT_VDIM, 1)
        ya_t = _attention(lam_pack, gcol, attn_consts, qt, k, vt)

        yb = _conformer_conv(hb, conv_dw_w[l].astype(F32), row(conv_dw_b[l]), row(conv_ln_g[l]),
                             row(conv_ln_b[l]), gavg, conv_pw_w[l].astype(BF16), row(conv_pw_b[l]))

        yc = _fourier_mix(hc, fnet_consts, _block_diag(fnet_w[l]).astype(BF16), row(fnet_b[l]))

        wcat = sgu_w[l].reshape(SGU_GROUPS * SGU_CHUNK, SGU_CHUNK).astype(BF16)
        bias_plane = jnp.repeat(sgu_b[l].T.astype(F32), gw // SGU_GROUPS, axis=1)
        yd = _spatial_gate(u, v, row(sgu_ln_g[l]), row(sgu_ln_b[l]), wcat, bias_plane)

        x = _outproj(x, ya_t, yb, yc, yd, w_out[l].reshape(4, gw, d).astype(BF16))
        x = _mlp(x, row(norm2_g[l]), w_up[l].astype(BF16), w_down[l].astype(BF16))

    return _final_norm(x, row(final_g))
```

```python
import functools
import math

import numpy as np
import jax
import jax.numpy as jnp
from jax import lax
from jax.experimental import pallas as pl
from jax.experimental.pallas import tpu as pltpu

D_MODEL = 1024
GROUP_WIDTH = 256
ATT_HEADS = 4
ATT_VDIM = 64
ATT_QK = 32
CONV_WIDTH = 31
CONV_GROUPS = 4
FNET_GROUPS = 4
SGU_CHUNK = 128
SGU_GROUPS = 4
D_FF = 4 * D_MODEL
EPS = 1e-6
LOG2E = 1.4426950408889634

KPAD = 128
POS_LANE = 2 * ATT_QK
N_SPLIT = 3
TQ = 256
TK = 256
V_ROWS = ATT_VDIM + 16
Q_STREAMS = 2
ITEMS_PER_TRIP = 16
TM = 512
VMEM_LIMIT = 56 * 1024 * 1024

BF16 = jnp.bfloat16
F32 = jnp.float32


def _params(sem, vmem=VMEM_LIMIT):
    return pltpu.CompilerParams(dimension_semantics=sem, vmem_limit_bytes=vmem)


def _full(shape):
    n = len(shape)
    return pl.BlockSpec(shape, lambda *_: (0,) * n)


def _inproj_kernel(x_ref, g_ref, wn_ref, wt_ref,
                   k_ref, hb_ref, hc_ref, u_ref, v_ref, qt_ref, vt_ref):
    x = x_ref[0]
    ms = jnp.mean(x * x, axis=-1, keepdims=True)
    xn = (x * lax.rsqrt(ms + EPS) * g_ref[...]).astype(BF16)
    nat = jnp.dot(xn, wn_ref[...], preferred_element_type=F32)
    tr = lax.dot_general(wt_ref[...], xn, (((1,), (1,)), ((), ())),
                         preferred_element_type=F32)
    kw = ATT_HEADS * KPAD
    k_ref[0] = nat[:, :kw].astype(BF16)
    hb_ref[0] = nat[:, kw:kw + 2 * GROUP_WIDTH]
    o = kw + 2 * GROUP_WIDTH
    hc_ref[0] = nat[:, o:o + GROUP_WIDTH].astype(BF16)
    u_ref[0] = nat[:, o + GROUP_WIDTH:o + 2 * GROUP_WIDTH]
    v_ref[0] = nat[:, o + 2 * GROUP_WIDTH:o + 3 * GROUP_WIDTH]
    qscale = (ATT_QK ** -0.5) * LOG2E
    for j in range(TM // TQ):
        qt_ref[0, j] = (tr[:kw, j * TQ:(j + 1) * TQ] * qscale).astype(BF16)
    for j in range(TM // TK):
        vt_ref[0, j] = tr[kw:, j * TK:(j + 1) * TK].astype(BF16)


def _inproj(x, g, wn, wt):
    b, s, d = x.shape
    kw = ATT_HEADS * KPAD
    nn = wn.shape[1]
    nt = wt.shape[0]
    grid = (b, s // TM)
    tok = lambda w: pl.BlockSpec((1, TM, w), lambda i, j: (i, j, 0))
    out_shape = (
        jax.ShapeDtypeStruct((b, s, kw), BF16),
        jax.ShapeDtypeStruct((b, s, 2 * GROUP_WIDTH), F32),
        jax.ShapeDtypeStruct((b, s, GROUP_WIDTH), BF16),
        jax.ShapeDtypeStruct((b, s, GROUP_WIDTH), F32),
        jax.ShapeDtypeStruct((b, s, GROUP_WIDTH), F32),
        jax.ShapeDtypeStruct((b, s // TQ, kw, TQ), BF16),
        jax.ShapeDtypeStruct((b, s // TK, GROUP_WIDTH, TK), BF16),
    )
    out_specs = (
        tok(kw), tok(2 * GROUP_WIDTH), tok(GROUP_WIDTH), tok(GROUP_WIDTH), tok(GROUP_WIDTH),
        pl.BlockSpec((1, TM // TQ, kw, TQ), lambda i, j: (i, j, 0, 0)),
        pl.BlockSpec((1, TM // TK, GROUP_WIDTH, TK), lambda i, j: (i, j, 0, 0)),
    )
    return pl.pallas_call(
        _inproj_kernel,
        out_shape=out_shape,
        grid=grid,
        in_specs=[tok(d), _full((1, d)), _full((d, nn)), _full((nt, d))],
        out_specs=out_specs,
        compiler_params=_params(("parallel", "parallel")),
        name="inproj",
    )(x, g, wn, wt)


def _attn_kernel(lam_ref, g_ref, crow_ref, posk_ref, posq_ref, dbias_ref,
                 qt_ref, k_ref, vt_ref, o_ref, kaug_ref, vaug_ref, rhs_ref, sa_ref, sb_ref):
    nq = qt_ref.shape[1]
    nk = vt_ref.shape[1]
    ns = Q_STREAMS

    for j in range(nk):
        kaug_ref[j] = k_ref[0, j * TK:(j + 1) * TK, :] + posk_ref[0]
        vaug_ref[j, :ATT_VDIM, :] = vt_ref[0, j]
        r = lax.broadcasted_iota(jnp.int32, (V_ROWS - ATT_VDIM, TK), 0)
        vaug_ref[j, ATT_VDIM:, :] = jnp.where(r == 0, 1.0, 0.0).astype(BF16)

    lp = lam_ref[...]
    lam_init = lp[4:5, 0:1]
    lam = (jnp.exp(jnp.sum(lp[0:1] * lp[1:2], axis=-1, keepdims=True))
           - jnp.exp(jnp.sum(lp[2:3] * lp[3:4], axis=-1, keepdims=True)) + lam_init)

    crow = crow_ref[0]
    posq = posq_ref[0]
    row = lax.broadcasted_iota(jnp.int32, (KPAD, TQ), 0)

    def key_block(u, qi):
        return jnp.where(u == 0, qi, u - 1 + jnp.where(u - 1 >= qi, 1, 0))

    def scores(buf, g, m, kj, rhs, diag):
        s = jnp.dot(kaug_ref[kj], rhs, preferred_element_type=F32)
        if diag:
            s = s + dbias_ref[0]
        buf[2 * g + m] = s
        return jnp.max(s, axis=0, keepdims=True)

    def consume(buf, g, m, kj, qi, mt, mrun, acc):
        off = crow * (-lax.convert_element_type(jnp.abs(kj - qi) * TK, F32))
        mnew = jnp.maximum(mrun, mt + off)
        alpha = jnp.exp2(mrun - mnew)
        p = jnp.exp2(buf[2 * g + m] - (mnew - off)).astype(BF16)
        acc = alpha * acc + jnp.dot(vaug_ref[kj], p, preferred_element_type=F32)
        return mnew, acc

    def prologue(qg):
        mts = []
        for g in range(ns):
            qi = qg * ns + g
            qt = qt_ref[0, qi]
            zero = jnp.zeros_like(qt)
            for m in range(2):
                qm = jnp.where((row >= ATT_QK * m) & (row < ATT_QK * (m + 1)), qt, zero)
                rhs_ref[(2 * g + 0) * 2 + m] = qm + posq
                rhs_ref[(2 * g + 1) * 2 + m] = qm - posq
                mts.append(scores(sa_ref, g, m, qi, qm, True))
        return tuple(mts)

    def produce(buf, qg, u):
        out = []
        for g in range(ns):
            qi = qg * ns + g
            kj = key_block(u, qi)
            side = jnp.where(kj > qi, 1, 0)
            for m in range(2):
                out.append(scores(buf, g, m, kj, rhs_ref[(2 * g + side) * 2 + m], False))
        return tuple(out)

    def finish(buf, qg, u, mts, mruns, accs):
        new_mruns, new_accs = [], []
        for g in range(ns):
            qi = qg * ns + g
            kp = key_block(u, qi)
            for m in range(2):
                i = 2 * g + m
                mr, ac = consume(buf, g, m, kp, qi, mts[i], mruns[i], accs[i])
                new_mruns.append(mr)
                new_accs.append(ac)
        return tuple(new_mruns), tuple(new_accs)

    def run_items(qg, first, count, mts, mruns, accs):
        for j in range(count):
            buf, prev = (sb_ref, sa_ref) if j % 2 == 0 else (sa_ref, sb_ref)
            new_mts = produce(buf, qg, first + j)
            mruns, accs = finish(prev, qg, first + j - 1, mts, mruns, accs)
            mts = new_mts
        return mts, mruns, accs

    n_loop = (nk - 1) // ITEMS_PER_TRIP
    n_tail = nk - 1 - n_loop * ITEMS_PER_TRIP

    def qgroup(qg, mts):
        init = (mts,
                tuple(jnp.full((1, TQ), -1e30, F32) for _ in range(2 * ns)),
                tuple(jnp.zeros((V_ROWS, TQ), F32) for _ in range(2 * ns)))

        def trip(t, carry):
            return run_items(qg, ITEMS_PER_TRIP * t + 1, ITEMS_PER_TRIP, *carry)

        carry = lax.fori_loop(0, n_loop, trip, init)
        mts, mruns, accs = run_items(qg, n_loop * ITEMS_PER_TRIP + 1, n_tail, *carry)
        last = sb_ref if (nk - 1) % 2 == 1 else sa_ref
        mruns, accs = finish(last, qg, nk - 1, mts, mruns, accs)
        nxt = prologue(jnp.minimum(qg + 1, nq // ns - 1))
        for g in range(ns):
            outs = [accs[2 * g + m][:ATT_VDIM] / accs[2 * g + m][ATT_VDIM:ATT_VDIM + 1]
                    for m in range(2)]
            o = outs[0] - lam * outs[1]
            ms = jnp.mean(o * o, axis=0, keepdims=True)
            o_ref[0, qg * ns + g] = (o * lax.rsqrt(ms + EPS) * g_ref[...]).astype(BF16)
        return nxt

    lax.fori_loop(0, nq // ns, qgroup, prologue(0))


def _attention(lam_pack, gcol, consts, qt, k, vt):
    b, nq = qt.shape[0], qt.shape[1]
    nk = vt.shape[1]
    s = k.shape[1]
    crow, posk, posq, dbias = consts
    head = lambda shape: pl.BlockSpec((1,) + shape, lambda i, h: (h, 0, 0))
    return pl.pallas_call(
        _attn_kernel,
        out_shape=jax.ShapeDtypeStruct((b, nq, GROUP_WIDTH, TQ), BF16),
        grid=(b, ATT_HEADS),
        in_specs=[
            _full(lam_pack.shape), _full(gcol.shape),
            head((1, TQ)), head((TK, KPAD)), head((KPAD, TQ)), head((TK, TQ)),
            pl.BlockSpec((1, nq, KPAD, TQ), lambda i, h: (i, 0, h, 0)),
            pl.BlockSpec((1, s, KPAD), lambda i, h: (i, 0, h)),
            pl.BlockSpec((1, nk, ATT_VDIM, TK), lambda i, h: (i, 0, h, 0)),
        ],
        out_specs=pl.BlockSpec((1, nq, ATT_VDIM, TQ), lambda i, h: (i, 0, h, 0)),
        scratch_shapes=[pltpu.VMEM((nk, TK, KPAD), BF16), pltpu.VMEM((nk, V_ROWS, TK), BF16),
                        pltpu.VMEM((4 * Q_STREAMS, KPAD, TQ), BF16),
                        pltpu.VMEM((2 * Q_STREAMS, TK, TQ), F32),
                        pltpu.VMEM((2 * Q_STREAMS, TK, TQ), F32)],
        compiler_params=_params(("parallel", "parallel")),
        name="diff_attention",
    )(lam_pack, gcol, crow, posk, posq, dbias, qt, k, vt)


def _attention_consts():
    slopes = np.array([2.0 ** (-8.0 * (i + 1) / ATT_HEADS) for i in range(ATT_HEADS)], np.float64)
    c = (slopes * LOG2E).astype(np.float32)
    parts = []
    rem = c.astype(np.float32)
    for _ in range(N_SPLIT):
        p = rem.astype(BF16).astype(np.float32)
        parts.append(p)
        rem = (rem - p).astype(np.float32)
    rk = np.arange(TK, dtype=np.float32)
    rq = np.arange(TQ, dtype=np.float32)
    posk = np.zeros((ATT_HEADS, TK, KPAD), np.float32)
    posq = np.zeros((ATT_HEADS, KPAD, TQ), np.float32)
    for t in range(N_SPLIT):
        posk[:, :, POS_LANE + t] = rk[None, :]
        posk[:, :, POS_LANE + N_SPLIT + t] = parts[t][:, None]
        posq[:, POS_LANE + t, :] = parts[t][:, None]
        posq[:, POS_LANE + N_SPLIT + t, :] = -rq[None, :]
    dbias = -c[:, None, None] * np.abs(rq[None, None, :] - rk[None, :, None])
    crow = np.broadcast_to(c[:, None, None], (ATT_HEADS, 1, TQ))
    return (jnp.asarray(crow, F32), jnp.asarray(posk, BF16), jnp.asarray(posq, BF16),
            jnp.asarray(dbias, F32))


CONV_PAD = 16
CONV_ROWS = 128


def _split_dot(x, w):
    hi = x.astype(BF16)
    lo = (x - hi.astype(F32)).astype(BF16)
    return (jnp.dot(hi, w, preferred_element_type=F32)
            + jnp.dot(lo, w, preferred_element_type=F32))


def _conv_kernel(hb_ref, dww_ref, dwb_ref, lng_ref, lnb_ref, gavg_ref, pww_ref, pwb_ref,
                 o_ref, z_ref, sh_ref):
    s = hb_ref.shape[1]
    half = CONV_WIDTH // 2
    zeros = jnp.zeros((CONV_PAD, GROUP_WIDTH), F32)
    z_ref[0:CONV_PAD, :] = zeros
    z_ref[CONV_PAD + s:CONV_PAD + s + CONV_PAD, :] = zeros

    def glu(i, _):
        r0 = pl.multiple_of(i * CONV_ROWS, CONV_ROWS)
        h = hb_ref[0, pl.ds(r0, CONV_ROWS), :]
        z_ref[pl.ds(CONV_PAD + r0, CONV_ROWS), :] = h[:, :GROUP_WIDTH] * jax.nn.sigmoid(h[:, GROUP_WIDTH:])
        return 0

    lax.fori_loop(0, s // CONV_ROWS, glu, 0)

    gavg = gavg_ref[...]

    def chunk(i, _):
        r0 = pl.multiple_of(i * CONV_ROWS, CONV_ROWS)
        acc = jnp.zeros((CONV_ROWS, GROUP_WIDTH), F32) + dwb_ref[...]
        win = z_ref[pl.ds(r0, CONV_ROWS + 2 * CONV_PAD), :]
        span = CONV_ROWS + 2 * CONV_PAD - 8
        for j in range(8):
            sh_ref[j] = win[j:j + span]
            for t in range(CONV_WIDTH):
                off = CONV_PAD - half + t
                if off % 8 == j:
                    a = off - j
                    acc = acc + dww_ref[t:t + 1, :] * sh_ref[j, a:a + CONV_ROWS, :]
        mu = _split_dot(acc, gavg)
        xc = acc - mu
        var = _split_dot(xc * xc, gavg)
        y = xc * lax.rsqrt(var + EPS) * lng_ref[...] + lnb_ref[...]
        y = y * jax.nn.sigmoid(y)
        out = jnp.dot(y.astype(BF16), pww_ref[...], preferred_element_type=F32) + pwb_ref[...]
        o_ref[0, pl.ds(r0, CONV_ROWS), :] = out.astype(BF16)
        return 0

    lax.fori_loop(0, s // CONV_ROWS, chunk, 0, unroll=4)


def _conformer_conv(hb, dww, dwb, lng, lnb, gavg, pww, pwb):
    b, s, _ = hb.shape
    return pl.pallas_call(
        _conv_kernel,
        out_shape=jax.ShapeDtypeStruct((b, s, GROUP_WIDTH), BF16),
        grid=(b,),
        in_specs=[pl.BlockSpec((1, s, 2 * GROUP_WIDTH), lambda i: (i, 0, 0)),
                  _full(dww.shape), _full(dwb.shape), _full(lng.shape), _full(lnb.shape),
                  _full(gavg.shape), _full(pww.shape), _full(pwb.shape)],
        out_specs=pl.BlockSpec((1, s, GROUP_WIDTH), lambda i: (i, 0, 0)),
        scratch_shapes=[pltpu.VMEM((s + 2 * CONV_PAD, GROUP_WIDTH), F32),
                        pltpu.VMEM((8, CONV_ROWS + 2 * CONV_PAD - 8, GROUP_WIDTH), F32)],
        compiler_params=_params(("parallel",)),
        name="conformer_conv",
    )(hb, dww, dwb, lng, lnb, gavg, pww, pwb)


FNET_ROWS = 512


def _fnet_kernel(c_ref, cmat_ref, m1_ref, m3_ref, twc_ref, tws_ref, w_ref, b_ref,
                 o_ref, z_ref, bb_ref, y_ref):
    s = c_ref.shape[1]
    n1 = m1_ref.shape[0] // 2
    n2 = m3_ref.shape[0]
    gw = GROUP_WIDTH
    rows = min(FNET_ROWS, s)

    nl = gw // 128

    for i in range(s // rows):
        z = jnp.dot(c_ref[0, i * rows:(i + 1) * rows, :], cmat_ref[...],
                    preferred_element_type=F32)
        for p in range(2 * nl):
            z_ref[p, i * rows:(i + 1) * rows, :] = z[:, p * 128:(p + 1) * 128]

    m1 = m1_ref[...]
    for j in range(n2):
        zcat = jnp.concatenate(
            [jnp.concatenate([z_ref[comp * nl + p, pl.ds(j, n1, stride=n2), :] for p in range(nl)],
                             axis=1) for comp in range(2)], axis=0)
        a = jnp.dot(m1, zcat.astype(BF16), preferred_element_type=F32)
        tc = twc_ref[j]
        ts = tws_ref[j]
        for p in range(nl):
            ar = a[:n1, p * 128:(p + 1) * 128]
            ai = a[n1:, p * 128:(p + 1) * 128]
            bb_ref[p, pl.ds(j, n1, stride=2 * n2), :] = ar * tc + ai * ts
            bb_ref[p, pl.ds(n2 + j, n1, stride=2 * n2), :] = ai * tc - ar * ts

    m3 = m3_ref[...]
    for k1 in range(n1):
        rhs = jnp.concatenate([bb_ref[p, k1 * 2 * n2:(k1 + 1) * 2 * n2, :] for p in range(nl)],
                              axis=1).astype(BF16)
        y = jnp.dot(m3, rhs, preferred_element_type=F32)
        for p in range(nl):
            y_ref[p, pl.ds(k1, n2, stride=n1), :] = y[:, p * 128:(p + 1) * 128]

    norm = 1.0 / math.sqrt(s * (gw // FNET_GROUPS))
    for i in range(s // rows):
        y = jnp.concatenate([y_ref[p, i * rows:(i + 1) * rows, :] for p in range(nl)], axis=1)
        out = jnp.dot((y * norm).astype(BF16), w_ref[...], preferred_element_type=F32) + b_ref[...]
        o_ref[0, i * rows:(i + 1) * rows, :] = out.astype(BF16)


def _fnet_consts(s):
    n1 = 1 << (int(math.log2(s)) // 2)
    n2 = s // n1
    cw = GROUP_WIDTH // FNET_GROUPS
    ang = 2.0 * np.pi * np.outer(np.arange(cw), np.arange(cw)) / cw
    eye = np.eye(FNET_GROUPS)
    cmat = np.concatenate([np.kron(eye, np.cos(ang)), -np.kron(eye, np.sin(ang))], axis=1)
    a1 = 2.0 * np.pi * np.outer(np.arange(n1), np.arange(n1)) / n1
    m1 = np.block([[np.cos(a1), np.sin(a1)], [-np.sin(a1), np.cos(a1)]])
    a2 = 2.0 * np.pi * np.outer(np.arange(n2), np.arange(n2)) / n2
    m3 = np.concatenate([np.cos(a2), np.sin(a2)], axis=1)
    at = 2.0 * np.pi * np.outer(np.arange(n2), np.arange(n1)) / s
    twc = np.broadcast_to(np.cos(at)[:, :, None], (n2, n1, 128))
    tws = np.broadcast_to(np.sin(at)[:, :, None], (n2, n1, 128))
    return (jnp.asarray(cmat, BF16), jnp.asarray(m1, BF16), jnp.asarray(m3, BF16),
            jnp.asarray(twc, F32), jnp.asarray(tws, F32))


def _fourier_mix(c, consts, w_bd, bias):
    b, s, gw = c.shape
    cmat, m1, m3, twc, tws = consts
    return pl.pallas_call(
        _fnet_kernel,
        out_shape=jax.ShapeDtypeStruct((b, s, gw), BF16),
        grid=(b,),
        in_specs=[pl.BlockSpec((1, s, gw), lambda i: (i, 0, 0)),
                  _full(cmat.shape), _full(m1.shape), _full(m3.shape),
                  _full(twc.shape), _full(tws.shape), _full(w_bd.shape), _full(bias.shape)],
        out_specs=pl.BlockSpec((1, s, gw), lambda i: (i, 0, 0)),
        scratch_shapes=[pltpu.VMEM((2 * gw // 128, s, 128), F32),
                        pltpu.VMEM((gw // 128, 2 * s, 128), F32),
                        pltpu.VMEM((gw // 128, s, 128), F32)],
        compiler_params=_params(("parallel",)),
        name="fourier_mix",
    )(c, cmat, m1, m3, twc, tws, w_bd, bias)


def _sgu_kernel(u_ref, v_ref, g_ref, b_ref, w_ref, bias_ref, o_ref):
    v = v_ref[0]
    mu = jnp.mean(v, axis=-1, keepdims=True)
    xc = v - mu
    var = jnp.mean(xc * xc, axis=-1, keepdims=True)
    vn = (xc * lax.rsqrt(var + EPS) * g_ref[...] + b_ref[...]).astype(BF16)
    cg = GROUP_WIDTH // SGU_GROUPS
    lane = lax.broadcasted_iota(jnp.int32, (SGU_CHUNK, GROUP_WIDTH), 1)
    w = w_ref[...]
    for c in range(TM // SGU_CHUNK):
        rs = slice(c * SGU_CHUNK, (c + 1) * SGU_CHUNK)
        r = jnp.dot(w, vn[rs], preferred_element_type=F32)
        sv = r[(SGU_GROUPS - 1) * SGU_CHUNK:]
        for g in range(SGU_GROUPS - 2, -1, -1):
            sv = jnp.where(lane < (g + 1) * cg, r[g * SGU_CHUNK:(g + 1) * SGU_CHUNK], sv)
        o_ref[0, rs, :] = (u_ref[0, rs, :] * (sv + bias_ref[...])).astype(BF16)


def _spatial_gate(u, v, g, b, wcat, bias_plane):
    bsz, s, gw = u.shape
    tok = pl.BlockSpec((1, TM, gw), lambda i, j: (i, j, 0))
    return pl.pallas_call(
        _sgu_kernel,
        out_shape=jax.ShapeDtypeStruct((bsz, s, gw), BF16),
        grid=(bsz, s // TM),
        in_specs=[tok, tok, _full(g.shape), _full(b.shape), _full(wcat.shape),
                  _full(bias_plane.shape)],
        out_specs=tok,
        compiler_params=_params(("parallel", "parallel")),
        name="spatial_gate",
    )(u, v, g, b, wcat, bias_plane)


def _outproj_kernel(x_ref, ya_ref, yb_ref, yc_ref, yd_ref, w_ref, o_ref):
    acc = (x_ref[0]
           + jnp.dot(yb_ref[0], w_ref[1], preferred_element_type=F32)
           + jnp.dot(yc_ref[0], w_ref[2], preferred_element_type=F32)
           + jnp.dot(yd_ref[0], w_ref[3], preferred_element_type=F32))
    for j in range(TM // TQ):
        ya = lax.dot_general(ya_ref[0, j], w_ref[0], (((0,), (0,)), ((), ())),
                             preferred_element_type=F32)
        o_ref[0, j * TQ:(j + 1) * TQ, :] = acc[j * TQ:(j + 1) * TQ] + ya


def _outproj(x, ya_t, yb, yc, yd, w):
    b, s, d = x.shape
    gw = GROUP_WIDTH
    tok = lambda width: pl.BlockSpec((1, TM, width), lambda i, j: (i, j, 0))
    return pl.pallas_call(
        _outproj_kernel,
        out_shape=jax.ShapeDtypeStruct((b, s, d), F32),
        grid=(b, s // TM),
        in_specs=[tok(d), pl.BlockSpec((1, TM // TQ, gw, TQ), lambda i, j: (i, j, 0, 0)),
                  tok(gw), tok(gw), tok(gw), _full(w.shape)],
        out_specs=tok(d),
        compiler_params=_params(("parallel", "parallel")),
        name="outproj",
    )(x, ya_t, yb, yc, yd, w)


FF_CHUNK = 1024


def _mlp_kernel(x_ref, g_ref, wu_ref, wd_ref, o_ref):
    x = x_ref[0]
    ms = jnp.mean(x * x, axis=-1, keepdims=True)
    xn = (x * lax.rsqrt(ms + EPS) * g_ref[...]).astype(BF16)
    acc = x
    for c in range(D_FF // FF_CHUNK):
        cs = slice(c * FF_CHUNK, (c + 1) * FF_CHUNK)
        h = jnp.maximum(jnp.dot(xn, wu_ref[:, cs], preferred_element_type=F32), 0.0)
        acc = acc + jnp.dot((h * h).astype(BF16), wd_ref[cs, :], preferred_element_type=F32)
    o_ref[0] = acc


def _mlp(x, g, wu, wd):
    b, s, d = x.shape
    tok = pl.BlockSpec((1, TM, d), lambda i, j: (i, j, 0))
    const = lambda shape: pl.BlockSpec(shape, lambda i, j: (0, 0), pipeline_mode=pl.Buffered(1))
    return pl.pallas_call(
        _mlp_kernel,
        out_shape=jax.ShapeDtypeStruct((b, s, d), F32),
        grid=(b, s // TM),
        in_specs=[tok, _full(g.shape), const(wu.shape), const(wd.shape)],
        out_specs=tok,
        compiler_params=_params(("parallel", "parallel")),
        name="mlp",
    )(x, g, wu, wd)


def _final_norm_kernel(x_ref, g_ref, o_ref):
    x = x_ref[0]
    ms = jnp.mean(x * x, axis=-1, keepdims=True)
    o_ref[0] = x * lax.rsqrt(ms + EPS) * g_ref[...]


def _final_norm(x, g):
    b, s, d = x.shape
    tok = pl.BlockSpec((1, TM, d), lambda i, j: (i, j, 0))
    return pl.pallas_call(
        _final_norm_kernel,
        out_shape=jax.ShapeDtypeStruct((b, s, d), F32),
        grid=(b, s // TM),
        in_specs=[tok, _full(g.shape)],
        out_specs=tok,
        compiler_params=_params(("parallel", "parallel")),
        name="final_norm",
    )(x, g)


def _block_diag(blocks):
    g, r, c = blocks.shape
    eye = jnp.eye(g, dtype=blocks.dtype)
    return (eye[:, None, :, None] * blocks[:, :, None, :]).reshape(g * r, g * c)


def _layer_weights(w_in_l):
    gw = GROUP_WIDTH
    d = w_in_l.shape[0]
    pad = lambda w: jnp.pad(w.reshape(d, ATT_HEADS, 2 * ATT_QK),
                            ((0, 0), (0, 0), (0, KPAD - 2 * ATT_QK))).reshape(d, ATT_HEADS * KPAD)
    wq, wk, wv = w_in_l[:, :gw], w_in_l[:, gw:2 * gw], w_in_l[:, 2 * gw:3 * gw]
    wn = jnp.concatenate([pad(wk), w_in_l[:, 3 * gw:]], axis=1).astype(BF16)
    wt = jnp.concatenate([pad(wq), wv], axis=1).T.astype(BF16)
    return wn, wt


def kernel(x, norm1_g, w_in, lam_q1, lam_k1, lam_q2, lam_k2, subln_g, conv_dw_w, conv_dw_b, conv_ln_g, conv_ln_b, conv_pw_w, conv_pw_b, fnet_w, fnet_b, sgu_ln_g, sgu_ln_b, sgu_w, sgu_b, w_out, norm2_g, w_up, w_down, final_g):
    bsz, seq, d = x.shape
    depth = w_in.shape[0]
    gw = GROUP_WIDTH
    assert seq % TM == 0 and d == D_MODEL

    attn_consts = _attention_consts()
    fnet_consts = _fnet_consts(seq)
    cg = gw // CONV_GROUPS
    gavg = jnp.asarray(np.kron(np.eye(CONV_GROUPS), np.full((cg, cg), 1.0 / cg)), BF16)
    row = lambda v: v.reshape(1, -1).astype(F32)

    for l in range(depth):
        wn, wt = _layer_weights(w_in[l])
        k, hb, hc, u, v, qt, vt = _inproj(x, row(norm1_g[l]), wn, wt)

        lam_init = 0.8 - 0.6 * math.exp(-0.3 * l)
        lam_pack = jnp.concatenate(
            [lam_q1[l][None], lam_k1[l][None], lam_q2[l][None], lam_k2[l][None],
             jnp.full((4, ATT_QK), lam_init, F32)], axis=0).astype(F32)
        gcol = (subln_g[l].astype(F32) * (1.0 - lam_init)).reshape(ATT_VDIM, 1)
        ya_t = _attention(lam_pack, gcol, attn_consts, qt, k, vt)

        yb = _conformer_conv(hb, conv_dw_w[l].astype(F32), row(conv_dw_b[l]), row(conv_ln_g[l]),
                             row(conv_ln_b[l]), gavg, conv_pw_w[l].astype(BF16), row(conv_pw_b[l]))

        yc = _fourier_mix(hc, fnet_consts, _block_diag(fnet_w[l]).astype(BF16), row(fnet_b[l]))

        wcat = sgu_w[l].reshape(SGU_GROUPS * SGU_CHUNK, SGU_CHUNK).astype(BF16)
        bias_plane = jnp.repeat(sgu_b[l].T.astype(F32), gw // SGU_GROUPS, axis=1)
        yd = _spatial_gate(u, v, row(sgu_ln_g[l]), row(sgu_ln_b[l]), wcat, bias_plane)

        x = _outproj(x, ya_t, yb, yc, yd, w_out[l].reshape(4, gw, d).astype(BF16))
        x = _mlp(x, row(norm2_g[l]), w_up[l].astype(BF16), w_down[l].astype(BF16))

    return _final_norm(x, row(final_g))
```

```python
import functools
import math

import numpy as np
import jax
import jax.numpy as jnp
from jax import lax
from jax.experimental import pallas as pl
from jax.experimental.pallas import tpu as pltpu

D_MODEL = 1024
GROUP_WIDTH = 256
ATT_HEADS = 4
ATT_VDIM = 64
ATT_QK = 32
CONV_WIDTH = 31
CONV_GROUPS = 4
FNET_GROUPS = 4
SGU_CHUNK = 128
SGU_GROUPS = 4
D_FF = 4 * D_MODEL
EPS = 1e-6
LOG2E = 1.4426950408889634

KPAD = 128
POS_LANE = 2 * ATT_QK
N_SPLIT = 3
TQ = 256
TK = 256
V_ROWS = ATT_VDIM + 16
Q_STREAMS = 2
ITEMS_PER_TRIP = 16
TM = 512
VMEM_LIMIT = 56 * 1024 * 1024

BF16 = jnp.bfloat16
F32 = jnp.float32


def _params(sem, vmem=VMEM_LIMIT):
    return pltpu.CompilerParams(dimension_semantics=sem, vmem_limit_bytes=vmem)


def _full(shape):
    n = len(shape)
    return pl.BlockSpec(shape, lambda *_: (0,) * n)


def _inproj_kernel(x_ref, g_ref, wn_ref, wt_ref,
                   k_ref, hb_ref, hc_ref, u_ref, v_ref, qt_ref, vt_ref):
    x = x_ref[0]
    ms = jnp.mean(x * x, axis=-1, keepdims=True)
    xn = (x * lax.rsqrt(ms + EPS) * g_ref[...]).astype(BF16)
    nat = jnp.dot(xn, wn_ref[...], preferred_element_type=F32)
    tr = lax.dot_general(wt_ref[...], xn, (((1,), (1,)), ((), ())),
                         preferred_element_type=F32)
    kw = ATT_HEADS * KPAD
    k_ref[0] = nat[:, :kw].astype(BF16)
    hb_ref[0] = nat[:, kw:kw + 2 * GROUP_WIDTH]
    o = kw + 2 * GROUP_WIDTH
    hc_ref[0] = nat[:, o:o + GROUP_WIDTH].astype(BF16)
    u_ref[0] = nat[:, o + GROUP_WIDTH:o + 2 * GROUP_WIDTH]
    v_ref[0] = nat[:, o + 2 * GROUP_WIDTH:o + 3 * GROUP_WIDTH]
    qscale = (ATT_QK ** -0.5) * LOG2E
    for j in range(TM // TQ):
        qt_ref[0, j] = (tr[:kw, j * TQ:(j + 1) * TQ] * qscale).astype(BF16)
    for j in range(TM // TK):
        vt_ref[0, j] = tr[kw:, j * TK:(j + 1) * TK].astype(BF16)


def _inproj(x, g, wn, wt):
    b, s, d = x.shape
    kw = ATT_HEADS * KPAD
    nn = wn.shape[1]
    nt = wt.shape[0]
    grid = (b, s // TM)
    tok = lambda w: pl.BlockSpec((1, TM, w), lambda i, j: (i, j, 0))
    out_shape = (
        jax.ShapeDtypeStruct((b, s, kw), BF16),
        jax.ShapeDtypeStruct((b, s, 2 * GROUP_WIDTH), F32),
        jax.ShapeDtypeStruct((b, s, GROUP_WIDTH), BF16),
        jax.ShapeDtypeStruct((b, s, GROUP_WIDTH), F32),
        jax.ShapeDtypeStruct((b, s, GROUP_WIDTH), F32),
        jax.ShapeDtypeStruct((b, s // TQ, kw, TQ), BF16),
        jax.ShapeDtypeStruct((b, s // TK, GROUP_WIDTH, TK), BF16),
    )
    out_specs = (
        tok(kw), tok(2 * GROUP_WIDTH), tok(GROUP_WIDTH), tok(GROUP_WIDTH), tok(GROUP_WIDTH),
        pl.BlockSpec((1, TM // TQ, kw, TQ), lambda i, j: (i, j, 0, 0)),
        pl.BlockSpec((1, TM // TK, GROUP_WIDTH, TK), lambda i, j: (i, j, 0, 0)),
    )
    return pl.pallas_call(
        _inproj_kernel,
        out_shape=out_shape,
        grid=grid,
        in_specs=[tok(d), _full((1, d)), _full((d, nn)), _full((nt, d))],
        out_specs=out_specs,
        compiler_params=_params(("parallel", "parallel")),
        name="inproj",
    )(x, g, wn, wt)


def _attn_kernel(lam_ref, g_ref, crow_ref, posk_ref, posq_ref, dbias_ref,
                 qt_ref, k_ref, vt_ref, o_ref, kaug_ref, vaug_ref, rhs_ref, sa_ref, sb_ref):
    nq = qt_ref.shape[1]
    nk = vt_ref.shape[1]
    ns = Q_STREAMS

    for j in range(nk):
        kaug_ref[j] = k_ref[0, j * TK:(j + 1) * TK, :] + posk_ref[0]
        vaug_ref[j, :ATT_VDIM, :] = vt_ref[0, j]
        r = lax.broadcasted_iota(jnp.int32, (V_ROWS - ATT_VDIM, TK), 0)
        vaug_ref[j, ATT_VDIM:, :] = jnp.where(r == 0, 1.0, 0.0).astype(BF16)

    lp = lam_ref[...]
    lam_init = lp[4:5, 0:1]
    lam = (jnp.exp(jnp.sum(lp[0:1] * lp[1:2], axis=-1, keepdims=True))
           - jnp.exp(jnp.sum(lp[2:3] * lp[3:4], axis=-1, keepdims=True)) + lam_init)

    crow = crow_ref[0]
    posq = posq_ref[0]
    row = lax.broadcasted_iota(jnp.int32, (KPAD, TQ), 0)

    def key_block(u, qi):
        return jnp.where(u == 0, qi, u - 1 + jnp.where(u - 1 >= qi, 1, 0))

    def scores(buf, g, m, kj, rhs, diag):
        s = jnp.dot(kaug_ref[kj], rhs, preferred_element_type=F32)
        if diag:
            s = s + dbias_ref[0]
        buf[2 * g + m] = s
        return jnp.max(s, axis=0, keepdims=True)

    def consume(buf, g, m, kj, qi, mt, mrun, acc):
        off = crow * (-lax.convert_element_type(jnp.abs(kj - qi) * TK, F32))
        mnew = jnp.maximum(mrun, mt + off)
        alpha = jnp.exp2(mrun - mnew)
        p = jnp.exp2(buf[2 * g + m] - (mnew - off)).astype(BF16)
        acc = alpha * acc + jnp.dot(vaug_ref[kj], p, preferred_element_type=F32)
        return mnew, acc

    def prologue(qg):
        mts = []
        for g in range(ns):
            qi = qg * ns + g
            qt = qt_ref[0, qi]
            zero = jnp.zeros_like(qt)
            for m in range(2):
                qm = jnp.where((row >= ATT_QK * m) & (row < ATT_QK * (m + 1)), qt, zero)
                rhs_ref[(2 * g + 0) * 2 + m] = qm + posq
                rhs_ref[(2 * g + 1) * 2 + m] = qm - posq
                mts.append(scores(sa_ref, g, m, qi, qm, True))
        return tuple(mts)

    def produce(buf, qg, u):
        out = []
        for g in range(ns):
            qi = qg * ns + g
            kj = key_block(u, qi)
            side = jnp.where(kj > qi, 1, 0)
            for m in range(2):
                out.append(scores(buf, g, m, kj, rhs_ref[(2 * g + side) * 2 + m], False))
        return tuple(out)

    def finish(buf, qg, u, mts, mruns, accs):
        new_mruns, new_accs = [], []
        for g in range(ns):
            qi = qg * ns + g
            kp = key_block(u, qi)
            for m in range(2):
                i = 2 * g + m
                mr, ac = consume(buf, g, m, kp, qi, mts[i], mruns[i], accs[i])
                new_mruns.append(mr)
                new_accs.append(ac)
        return tuple(new_mruns), tuple(new_accs)

    def run_items(qg, first, count, mts, mruns, accs):
        for j in range(count):
            buf, prev = (sb_ref, sa_ref) if j % 2 == 0 else (sa_ref, sb_ref)
            new_mts = produce(buf, qg, first + j)
            mruns, accs = finish(prev, qg, first + j - 1, mts, mruns, accs)
            mts = new_mts
        return mts, mruns, accs

    n_loop = (nk - 1) // ITEMS_PER_TRIP
    n_tail = nk - 1 - n_loop * ITEMS_PER_TRIP

    def qgroup(qg, mts):
        init = (mts,
                tuple(jnp.full((1, TQ), -1e30, F32) for _ in range(2 * ns)),
                tuple(jnp.zeros((V_ROWS, TQ), F32) for _ in range(2 * ns)))

        def trip(t, carry):
            return run_items(qg, ITEMS_PER_TRIP * t + 1, ITEMS_PER_TRIP, *carry)

        carry = lax.fori_loop(0, n_loop, trip, init)
        mts, mruns, accs = run_items(qg, n_loop * ITEMS_PER_TRIP + 1, n_tail, *carry)
        last = sb_ref if (nk - 1) % 2 == 1 else sa_ref
        mruns, accs = finish(last, qg, nk - 1, mts, mruns, accs)
        nxt = prologue(jnp.minimum(qg + 1, nq // ns - 1))
        for g in range(ns):
            outs = [accs[2 * g + m][:ATT_VDIM] / accs[2 * g + m][ATT_VDIM:ATT_VDIM + 1]
                    for m in range(2)]
            o = outs[0] - lam * outs[1]
            ms = jnp.mean(o * o, axis=0, keepdims=True)
            o_ref[0, qg * ns + g] = (o * lax.rsqrt(ms + EPS) * g_ref[...]).astype(BF16)
        return nxt

    lax.fori_loop(0, nq // ns, qgroup, prologue(0))


def _attention(lam_pack, gcol, consts, qt, k, vt):
    b, nq = qt.shape[0], qt.shape[1]
    nk = vt.shape[1]
    s = k.shape[1]
    crow, posk, posq, dbias = consts
    head = lambda shape: pl.BlockSpec((1,) + shape, lambda i, h: (h, 0, 0))
    return pl.pallas_call(
        _attn_kernel,
        out_shape=jax.ShapeDtypeStruct((b, nq, GROUP_WIDTH, TQ), BF16),
        grid=(b, ATT_HEADS),
        in_specs=[
            _full(lam_pack.shape), _full(gcol.shape),
            head((1, TQ)), head((TK, KPAD)), head((KPAD, TQ)), head((TK, TQ)),
            pl.BlockSpec((1, nq, KPAD, TQ), lambda i, h: (i, 0, h, 0)),
            pl.BlockSpec((1, s, KPAD), lambda i, h: (i, 0, h)),
            pl.BlockSpec((1, nk, ATT_VDIM, TK), lambda i, h: (i, 0, h, 0)),
        ],
        out_specs=pl.BlockSpec((1, nq, ATT_VDIM, TQ), lambda i, h: (i, 0, h, 0)),
        scratch_shapes=[pltpu.VMEM((nk, TK, KPAD), BF16), pltpu.VMEM((nk, V_ROWS, TK), BF16),
                        pltpu.VMEM((4 * Q_STREAMS, KPAD, TQ), BF16),
                        pltpu.VMEM((2 * Q_STREAMS, TK, TQ), F32),
                        pltpu.VMEM((2 * Q_STREAMS, TK, TQ), F32)],
        compiler_params=_params(("parallel", "parallel")),
        name="diff_attention",
    )(lam_pack, gcol, crow, posk, posq, dbias, qt, k, vt)


def _attention_consts():
    slopes = np.array([2.0 ** (-8.0 * (i + 1) / ATT_HEADS) for i in range(ATT_HEADS)], np.float64)
    c = (slopes * LOG2E).astype(np.float32)
    parts = []
    rem = c.astype(np.float32)
    for _ in range(N_SPLIT):
        p = rem.astype(BF16).astype(np.float32)
        parts.append(p)
        rem = (rem - p).astype(np.float32)
    rk = np.arange(TK, dtype=np.float32)
    rq = np.arange(TQ, dtype=np.float32)
    posk = np.zeros((ATT_HEADS, TK, KPAD), np.float32)
    posq = np.zeros((ATT_HEADS, KPAD, TQ), np.float32)
    for t in range(N_SPLIT):
        posk[:, :, POS_LANE + t] = rk[None, :]
        posk[:, :, POS_LANE + N_SPLIT + t] = parts[t][:, None]
        posq[:, POS_LANE + t, :] = parts[t][:, None]
        posq[:, POS_LANE + N_SPLIT + t, :] = -rq[None, :]
    dbias = -c[:, None, None] * np.abs(rq[None, None, :] - rk[None, :, None])
    crow = np.broadcast_to(c[:, None, None], (ATT_HEADS, 1, TQ))
    return (jnp.asarray(crow, F32), jnp.asarray(posk, BF16), jnp.asarray(posq, BF16),
            jnp.asarray(dbias, F32))


CONV_PAD = 16
CONV_ROWS = 128


def _split_dot(x, w):
    hi = x.astype(BF16)
    lo = (x - hi.astype(F32)).astype(BF16)
    return (jnp.dot(hi, w, preferred_element_type=F32)
            + jnp.dot(lo, w, preferred_element_type=F32))


def _conv_kernel(hb_ref, dww_ref, dwb_ref, lng_ref, lnb_ref, gavg_ref, pww_ref, pwb_ref,
                 o_ref, z_ref, sh_ref):
    s = hb_ref.shape[1]
    half = CONV_WIDTH // 2
    zeros = jnp.zeros((CONV_PAD, GROUP_WIDTH), F32)
    z_ref[0:CONV_PAD, :] = zeros
    z_ref[CONV_PAD + s:CONV_PAD + s + CONV_PAD, :] = zeros

    def glu(i, _):
        r0 = pl.multiple_of(i * CONV_ROWS, CONV_ROWS)
        h = hb_ref[0, pl.ds(r0, CONV_ROWS), :]
        z_ref[pl.ds(CONV_PAD + r0, CONV_ROWS), :] = h[:, :GROUP_WIDTH] * jax.nn.sigmoid(h[:, GROUP_WIDTH:])
        return 0

    lax.fori_loop(0, s // CONV_ROWS, glu, 0)

    gavg = gavg_ref[...]

    def chunk(i, _):
        r0 = pl.multiple_of(i * CONV_ROWS, CONV_ROWS)
        acc = jnp.zeros((CONV_ROWS, GROUP_WIDTH), F32) + dwb_ref[...]
        win = z_ref[pl.ds(r0, CONV_ROWS + 2 * CONV_PAD), :]
        span = CONV_ROWS + 2 * CONV_PAD - 8
        for j in range(8):
            sh_ref[j] = win[j:j + span]
            for t in range(CONV_WIDTH):
                off = CONV_PAD - half + t
                if off % 8 == j:
                    a = off - j
                    acc = acc + dww_ref[t:t + 1, :] * sh_ref[j, a:a + CONV_ROWS, :]
        mu = _split_dot(acc, gavg)
        xc = acc - mu
        var = _split_dot(xc * xc, gavg)
        y = xc * lax.rsqrt(var + EPS) * lng_ref[...] + lnb_ref[...]
        y = y * jax.nn.sigmoid(y)
        out = jnp.dot(y.astype(BF16), pww_ref[...], preferred_element_type=F32) + pwb_ref[...]
        o_ref[0, pl.ds(r0, CONV_ROWS), :] = out.astype(BF16)
        return 0

    lax.fori_loop(0, s // CONV_ROWS, chunk, 0, unroll=4)


def _conformer_conv(hb, dww, dwb, lng, lnb, gavg, pww, pwb):
    b, s, _ = hb.shape
    return pl.pallas_call(
        _conv_kernel,
        out_shape=jax.ShapeDtypeStruct((b, s, GROUP_WIDTH), BF16),
        grid=(b,),
        in_specs=[pl.BlockSpec((1, s, 2 * GROUP_WIDTH), lambda i: (i, 0, 0)),
                  _full(dww.shape), _full(dwb.shape), _full(lng.shape), _full(lnb.shape),
                  _full(gavg.shape), _full(pww.shape), _full(pwb.shape)],
        out_specs=pl.BlockSpec((1, s, GROUP_WIDTH), lambda i: (i, 0, 0)),
        scratch_shapes=[pltpu.VMEM((s + 2 * CONV_PAD, GROUP_WIDTH), F32),
                        pltpu.VMEM((8, CONV_ROWS + 2 * CONV_PAD - 8, GROUP_WIDTH), F32)],
        compiler_params=_params(("parallel",)),
        name="conformer_conv",
    )(hb, dww, dwb, lng, lnb, gavg, pww, pwb)


FNET_ROWS = 512
FNET_PAD = 8


def _fnet_kernel(c_ref, cmat_ref, m1_ref, m3_ref, twc_ref, tws_ref, w_ref, b_ref,
                 o_ref, z_ref, bb_ref, y_ref):
    s = c_ref.shape[1]
    n1 = m1_ref.shape[0] // 2
    n2 = m3_ref.shape[0]
    gw = GROUP_WIDTH
    rows = min(FNET_ROWS, s)

    nl = gw // 128
    pz, pb, py = n2 + FNET_PAD, 2 * n2 + FNET_PAD, n1 + FNET_PAD

    for i in range(s // rows):
        z = jnp.dot(c_ref[0, i * rows:(i + 1) * rows, :], cmat_ref[...],
                    preferred_element_type=F32)
        for q in range(rows // n2):
            i1 = i * (rows // n2) + q
            for p in range(2 * nl):
                z_ref[p, i1 * pz:i1 * pz + n2, :] = z[q * n2:(q + 1) * n2, p * 128:(p + 1) * 128]

    m1 = m1_ref[...]
    for j in range(n2):
        zcat = jnp.concatenate(
            [jnp.concatenate([z_ref[comp * nl + p, pl.ds(j, n1, stride=pz), :] for p in range(nl)],
                             axis=1) for comp in range(2)], axis=0)
        a = jnp.dot(m1, zcat.astype(BF16), preferred_element_type=F32)
        tc = twc_ref[j]
        ts = tws_ref[j]
        for p in range(nl):
            ar = a[:n1, p * 128:(p + 1) * 128]
            ai = a[n1:, p * 128:(p + 1) * 128]
            bb_ref[p, pl.ds(j, n1, stride=pb), :] = ar * tc + ai * ts
            bb_ref[p, pl.ds(n2 + j, n1, stride=pb), :] = ai * tc - ar * ts

    m3 = m3_ref[...]
    for k1 in range(n1):
        rhs = jnp.concatenate([bb_ref[p, k1 * pb:k1 * pb + 2 * n2, :] for p in range(nl)],
                              axis=1).astype(BF16)
        y = jnp.dot(m3, rhs, preferred_element_type=F32)
        for p in range(nl):
            y_ref[p, pl.ds(k1, n2, stride=py), :] = y[:, p * 128:(p + 1) * 128]

    norm = 1.0 / math.sqrt(s * (gw // FNET_GROUPS))
    for i in range(s // rows):
        y = jnp.concatenate(
            [jnp.concatenate([y_ref[p, k2 * py:k2 * py + n1, :] for p in range(nl)], axis=1)
             for k2 in range(i * (rows // n1), (i + 1) * (rows // n1))], axis=0)
        out = jnp.dot((y * norm).astype(BF16), w_ref[...], preferred_element_type=F32) + b_ref[...]
        o_ref[0, i * rows:(i + 1) * rows, :] = out.astype(BF16)


def _fnet_consts(s):
    n1 = 1 << (int(math.log2(s)) // 2)
    n2 = s // n1
    cw = GROUP_WIDTH // FNET_GROUPS
    ang = 2.0 * np.pi * np.outer(np.arange(cw), np.arange(cw)) / cw
    eye = np.eye(FNET_GROUPS)
    cmat = np.concatenate([np.kron(eye, np.cos(ang)), -np.kron(eye, np.sin(ang))], axis=1)
    a1 = 2.0 * np.pi * np.outer(np.arange(n1), np.arange(n1)) / n1
    m1 = np.block([[np.cos(a1), np.sin(a1)], [-np.sin(a1), np.cos(a1)]])
    a2 = 2.0 * np.pi * np.outer(np.arange(n2), np.arange(n2)) / n2
    m3 = np.concatenate([np.cos(a2), np.sin(a2)], axis=1)
    at = 2.0 * np.pi * np.outer(np.arange(n2), np.arange(n1)) / s
    twc = np.broadcast_to(np.cos(at)[:, :, None], (n2, n1, 128))
    tws = np.broadcast_to(np.sin(at)[:, :, None], (n2, n1, 128))
    return (jnp.asarray(cmat, BF16), jnp.asarray(m1, BF16), jnp.asarray(m3, BF16),
            jnp.asarray(twc, F32), jnp.asarray(tws, F32))


def _fourier_mix(c, consts, w_bd, bias):
    b, s, gw = c.shape
    cmat, m1, m3, twc, tws = consts
    n1, n2 = m1.shape[0] // 2, m3.shape[0]
    return pl.pallas_call(
        _fnet_kernel,
        out_shape=jax.ShapeDtypeStruct((b, s, gw), BF16),
        grid=(b,),
        in_specs=[pl.BlockSpec((1, s, gw), lambda i: (i, 0, 0)),
                  _full(cmat.shape), _full(m1.shape), _full(m3.shape),
                  _full(twc.shape), _full(tws.shape), _full(w_bd.shape), _full(bias.shape)],
        out_specs=pl.BlockSpec((1, s, gw), lambda i: (i, 0, 0)),
        scratch_shapes=[pltpu.VMEM((2 * gw // 128, n1 * (n2 + FNET_PAD), 128), F32),
                        pltpu.VMEM((gw // 128, n1 * (2 * n2 + FNET_PAD), 128), F32),
                        pltpu.VMEM((gw // 128, n2 * (n1 + FNET_PAD), 128), F32)],
        compiler_params=_params(("parallel",)),
        name="fourier_mix",
    )(c, cmat, m1, m3, twc, tws, w_bd, bias)


def _sgu_kernel(u_ref, v_ref, g_ref, b_ref, w_ref, bias_ref, o_ref):
    v = v_ref[0]
    mu = jnp.mean(v, axis=-1, keepdims=True)
    xc = v - mu
    var = jnp.mean(xc * xc, axis=-1, keepdims=True)
    vn = (xc * lax.rsqrt(var + EPS) * g_ref[...] + b_ref[...]).astype(BF16)
    cg = GROUP_WIDTH // SGU_GROUPS
    lane = lax.broadcasted_iota(jnp.int32, (SGU_CHUNK, GROUP_WIDTH), 1)
    w = w_ref[...]
    for c in range(TM // SGU_CHUNK):
        rs = slice(c * SGU_CHUNK, (c + 1) * SGU_CHUNK)
        r = jnp.dot(w, vn[rs], preferred_element_type=F32)
        sv = r[(SGU_GROUPS - 1) * SGU_CHUNK:]
        for g in range(SGU_GROUPS - 2, -1, -1):
            sv = jnp.where(lane < (g + 1) * cg, r[g * SGU_CHUNK:(g + 1) * SGU_CHUNK], sv)
        o_ref[0, rs, :] = (u_ref[0, rs, :] * (sv + bias_ref[...])).astype(BF16)


def _spatial_gate(u, v, g, b, wcat, bias_plane):
    bsz, s, gw = u.shape
    tok = pl.BlockSpec((1, TM, gw), lambda i, j: (i, j, 0))
    return pl.pallas_call(
        _sgu_kernel,
        out_shape=jax.ShapeDtypeStruct((bsz, s, gw), BF16),
        grid=(bsz, s // TM),
        in_specs=[tok, tok, _full(g.shape), _full(b.shape), _full(wcat.shape),
                  _full(bias_plane.shape)],
        out_specs=tok,
        compiler_params=_params(("parallel", "parallel")),
        name="spatial_gate",
    )(u, v, g, b, wcat, bias_plane)


def _outproj_kernel(x_ref, ya_ref, yb_ref, yc_ref, yd_ref, w_ref, o_ref):
    acc = (x_ref[0]
           + jnp.dot(yb_ref[0], w_ref[1], preferred_element_type=F32)
           + jnp.dot(yc_ref[0], w_ref[2], preferred_element_type=F32)
           + jnp.dot(yd_ref[0], w_ref[3], preferred_element_type=F32))
    for j in range(TM // TQ):
        ya = lax.dot_general(ya_ref[0, j], w_ref[0], (((0,), (0,)), ((), ())),
                             preferred_element_type=F32)
        o_ref[0, j * TQ:(j + 1) * TQ, :] = acc[j * TQ:(j + 1) * TQ] + ya


def _outproj(x, ya_t, yb, yc, yd, w):
    b, s, d = x.shape
    gw = GROUP_WIDTH
    tok = lambda width: pl.BlockSpec((1, TM, width), lambda i, j: (i, j, 0))
    return pl.pallas_call(
        _outproj_kernel,
        out_shape=jax.ShapeDtypeStruct((b, s, d), F32),
        grid=(b, s // TM),
        in_specs=[tok(d), pl.BlockSpec((1, TM // TQ, gw, TQ), lambda i, j: (i, j, 0, 0)),
                  tok(gw), tok(gw), tok(gw), _full(w.shape)],
        out_specs=tok(d),
        compiler_params=_params(("parallel", "parallel")),
        name="outproj",
    )(x, ya_t, yb, yc, yd, w)


FF_CHUNK = 1024


def _mlp_kernel(x_ref, g_ref, wu_ref, wd_ref, o_ref):
    x = x_ref[0]
    ms = jnp.mean(x * x, axis=-1, keepdims=True)
    xn = (x * lax.rsqrt(ms + EPS) * g_ref[...]).astype(BF16)
    acc = x
    for c in range(D_FF // FF_CHUNK):
        cs = slice(c * FF_CHUNK, (c + 1) * FF_CHUNK)
        h = jnp.maximum(jnp.dot(xn, wu_ref[:, cs], preferred_element_type=F32), 0.0)
        acc = acc + jnp.dot((h * h).astype(BF16), wd_ref[cs, :], preferred_element_type=F32)
    o_ref[0] = acc


def _mlp(x, g, wu, wd):
    b, s, d = x.shape
    tok = pl.BlockSpec((1, TM, d), lambda i, j: (i, j, 0))
    const = lambda shape: pl.BlockSpec(shape, lambda i, j: (0, 0), pipeline_mode=pl.Buffered(1))
    return pl.pallas_call(
        _mlp_kernel,
        out_shape=jax.ShapeDtypeStruct((b, s, d), F32),
        grid=(b, s // TM),
        in_specs=[tok, _full(g.shape), const(wu.shape), const(wd.shape)],
        out_specs=tok,
        compiler_params=_params(("parallel", "parallel")),
        name="mlp",
    )(x, g, wu, wd)


def _final_norm_kernel(x_ref, g_ref, o_ref):
    x = x_ref[0]
    ms = jnp.mean(x * x, axis=-1, keepdims=True)
    o_ref[0] = x * lax.rsqrt(ms + EPS) * g_ref[...]


def _final_norm(x, g):
    b, s, d = x.shape
    tok = pl.BlockSpec((1, TM, d), lambda i, j: (i, j, 0))
    return pl.pallas_call(
        _final_norm_kernel,
        out_shape=jax.ShapeDtypeStruct((b, s, d), F32),
        grid=(b, s // TM),
        in_specs=[tok, _full(g.shape)],
        out_specs=tok,
        compiler_params=_params(("parallel", "parallel")),
        name="final_norm",
    )(x, g)


def _block_diag(blocks):
    g, r, c = blocks.shape
    eye = jnp.eye(g, dtype=blocks.dtype)
    return (eye[:, None, :, None] * blocks[:, :, None, :]).reshape(g * r, g * c)


def _layer_weights(w_in_l):
    gw = GROUP_WIDTH
    d = w_in_l.shape[0]
    pad = lambda w: jnp.pad(w.reshape(d, ATT_HEADS, 2 * ATT_QK),
                            ((0, 0), (0, 0), (0, KPAD - 2 * ATT_QK))).reshape(d, ATT_HEADS * KPAD)
    wq, wk, wv = w_in_l[:, :gw], w_in_l[:, gw:2 * gw], w_in_l[:, 2 * gw:3 * gw]
    wn = jnp.concatenate([pad(wk), w_in_l[:, 3 * gw:]], axis=1).astype(BF16)
    wt = jnp.concatenate([pad(wq), wv], axis=1).T.astype(BF16)
    return wn, wt


def kernel(x, norm1_g, w_in, lam_q1, lam_k1, lam_q2, lam_k2, subln_g, conv_dw_w, conv_dw_b, conv_ln_g, conv_ln_b, conv_pw_w, conv_pw_b, fnet_w, fnet_b, sgu_ln_g, sgu_ln_b, sgu_w, sgu_b, w_out, norm2_g, w_up, w_down, final_g):
    bsz, seq, d = x.shape
    depth = w_in.shape[0]
    gw = GROUP_WIDTH
    assert seq % TM == 0 and d == D_MODEL

    attn_consts = _attention_consts()
    fnet_consts = _fnet_consts(seq)
    cg = gw // CONV_GROUPS
    gavg = jnp.asarray(np.kron(np.eye(CONV_GROUPS), np.full((cg, cg), 1.0 / cg)), BF16)
    row = lambda v: v.reshape(1, -1).astype(F32)

    for l in range(depth):
        wn, wt = _layer_weights(w_in[l])
        k, hb, hc, u, v, qt, vt = _inproj(x, row(norm1_g[l]), wn, wt)

        lam_init = 0.8 - 0.6 * math.exp(-0.3 * l)
        lam_pack = jnp.concatenate(
            [lam_q1[l][None], lam_k1[l][None], lam_q2[l][None], lam_k2[l][None],
             jnp.full((4, ATT_QK), lam_init, F32)], axis=0).astype(F32)
        gcol = (subln_g[l].astype(F32) * (1.0 - lam_init)).reshape(ATT_VDIM, 1)
        ya_t = _attention(lam_pack, gcol, attn_consts, qt, k, vt)

        yb = _conformer_conv(hb, conv_dw_w[l].astype(F32), row(conv_dw_b[l]), row(conv_ln_g[l]),
                             row(conv_ln_b[l]), gavg, conv_pw_w[l].astype(BF16), row(conv_pw_b[l]))

        yc = _fourier_mix(hc, fnet_consts, _block_diag(fnet_w[l]).astype(BF16), row(fnet_b[l]))

        wcat = sgu_w[l].reshape(SGU_GROUPS * SGU_CHUNK, SGU_CHUNK).astype(BF16)
        bias_plane = jnp.repeat(sgu_b[l].T.astype(F32), gw // SGU_GROUPS, axis=1)
        yd = _spatial_gate(u, v, row(sgu_ln_g[l]), row(sgu_ln_b[l]), wcat, bias_plane)

        x = _outproj(x, ya_t, yb, yc, yd, w_out[l].reshape(4, gw, d).astype(BF16))
        x = _mlp(x, row(norm2_g[l]), w_up[l].astype(BF16), w_down[l].astype(BF16))

    return _final_norm(x, row(final_g))
```

```python
import functools
import math

import numpy as np
import jax
import jax.numpy as jnp
from jax import lax
from jax.experimental import pallas as pl
from jax.experimental.pallas import tpu as pltpu

D_MODEL = 1024
GROUP_WIDTH = 256
ATT_HEADS = 4
ATT_VDIM = 64
ATT_QK = 32
CONV_WIDTH = 31
CONV_GROUPS = 4
FNET_GROUPS = 4
SGU_CHUNK = 128
SGU_GROUPS = 4
D_FF = 4 * D_MODEL
EPS = 1e-6
LOG2E = 1.4426950408889634

KPAD = 128
POS_LANE = 2 * ATT_QK
N_SPLIT = 3
TQ = 256
TK = 256
V_ROWS = ATT_VDIM + 16
Q_STREAMS = 2
ITEMS_PER_TRIP = 16
TM = 512
VMEM_LIMIT = 56 * 1024 * 1024

BF16 = jnp.bfloat16
F32 = jnp.float32


def _params(sem, vmem=VMEM_LIMIT):
    return pltpu.CompilerParams(dimension_semantics=sem, vmem_limit_bytes=vmem)


def _full(shape):
    n = len(shape)
    return pl.BlockSpec(shape, lambda *_: (0,) * n)


def _sgu(u, v, g_ref, b_ref, w_ref, bias_ref, o_ref):
    mu = jnp.mean(v, axis=-1, keepdims=True)
    xc = v - mu
    var = jnp.mean(xc * xc, axis=-1, keepdims=True)
    vn = (xc * lax.rsqrt(var + EPS) * g_ref[...] + b_ref[...]).astype(BF16)
    cg = GROUP_WIDTH // SGU_GROUPS
    lane = lax.broadcasted_iota(jnp.int32, (SGU_CHUNK, GROUP_WIDTH), 1)
    w = w_ref[...]
    for c in range(TM // SGU_CHUNK):
        rs = slice(c * SGU_CHUNK, (c + 1) * SGU_CHUNK)
        r = jnp.dot(w, vn[rs], preferred_element_type=F32)
        sv = r[(SGU_GROUPS - 1) * SGU_CHUNK:]
        for g in range(SGU_GROUPS - 2, -1, -1):
            sv = jnp.where(lane < (g + 1) * cg, r[g * SGU_CHUNK:(g + 1) * SGU_CHUNK], sv)
        o_ref[0, rs, :] = (u[rs] * (sv + bias_ref[...])).astype(BF16)


def _inproj_kernel(x_ref, g_ref, wn_ref, wt_ref, sg_ref, sb_ref, sw_ref, sbias_ref,
                   k_ref, hb_ref, hc_ref, yd_ref, qt_ref, vt_ref):
    gw = GROUP_WIDTH
    x = x_ref[0]
    ms = jnp.mean(x * x, axis=-1, keepdims=True)
    xn = (x * lax.rsqrt(ms + EPS) * g_ref[...]).astype(BF16)
    nat = jnp.dot(xn, wn_ref[...], preferred_element_type=F32)
    tr = lax.dot_general(wt_ref[...], xn, (((1,), (1,)), ((), ())),
                         preferred_element_type=F32)
    hw = 2 * ATT_QK
    kz = jnp.zeros((TM, KPAD - hw), F32)
    k_ref[0] = jnp.concatenate(
        [piece for h in range(ATT_HEADS) for piece in (nat[:, h * hw:(h + 1) * hw], kz)],
        axis=1).astype(BF16)
    hb_ref[0] = nat[:, gw:3 * gw]
    hc_ref[0] = nat[:, 3 * gw:4 * gw].astype(BF16)
    _sgu(nat[:, 4 * gw:5 * gw], nat[:, 5 * gw:6 * gw], sg_ref, sb_ref, sw_ref, sbias_ref, yd_ref)
    qscale = (ATT_QK ** -0.5) * LOG2E
    qz = jnp.zeros((KPAD - hw, TM), F32)
    q = jnp.concatenate(
        [piece for h in range(ATT_HEADS) for piece in (tr[h * hw:(h + 1) * hw] * qscale, qz)],
        axis=0).astype(BF16)
    for j in range(TM // TQ):
        qt_ref[0, j] = q[:, j * TQ:(j + 1) * TQ]
    for j in range(TM // TK):
        vt_ref[0, j] = tr[gw:, j * TK:(j + 1) * TK].astype(BF16)


def _inproj(x, g, wn, wt, sgu_g, sgu_b, sgu_w, sgu_bias):
    b, s, d = x.shape
    kw = ATT_HEADS * KPAD
    gw = GROUP_WIDTH
    grid = (b, s // TM)
    tok = lambda w: pl.BlockSpec((1, TM, w), lambda i, j: (i, j, 0))
    out_shape = (
        jax.ShapeDtypeStruct((b, s, kw), BF16),
        jax.ShapeDtypeStruct((b, s, 2 * gw), F32),
        jax.ShapeDtypeStruct((b, s, gw), BF16),
        jax.ShapeDtypeStruct((b, s, gw), BF16),
        jax.ShapeDtypeStruct((b, s // TQ, kw, TQ), BF16),
        jax.ShapeDtypeStruct((b, s // TK, gw, TK), BF16),
    )
    out_specs = (
        tok(kw), tok(2 * gw), tok(gw), tok(gw),
        pl.BlockSpec((1, TM // TQ, kw, TQ), lambda i, j: (i, j, 0, 0)),
        pl.BlockSpec((1, TM // TK, gw, TK), lambda i, j: (i, j, 0, 0)),
    )
    consts = (g, wn, wt, sgu_g, sgu_b, sgu_w, sgu_bias)
    return pl.pallas_call(
        _inproj_kernel,
        out_shape=out_shape,
        grid=grid,
        in_specs=[tok(d)] + [_full(c.shape) for c in consts],
        out_specs=out_specs,
        compiler_params=_params(("parallel", "parallel")),
        name="inproj",
    )(x, *consts)


def _attn_kernel(lam_ref, g_ref, crow_ref, posk_ref, posq_ref, dbias_ref,
                 qt_ref, k_ref, vt_ref, o_ref, kaug_ref, vaug_ref, rhs_ref, sa_ref, sb_ref):
    nq = qt_ref.shape[1]
    nk = vt_ref.shape[1]
    ns = Q_STREAMS

    for j in range(nk):
        kaug_ref[j] = k_ref[0, j * TK:(j + 1) * TK, :] + posk_ref[0]
        vaug_ref[j, :ATT_VDIM, :] = vt_ref[0, j]
        r = lax.broadcasted_iota(jnp.int32, (V_ROWS - ATT_VDIM, TK), 0)
        vaug_ref[j, ATT_VDIM:, :] = jnp.where(r == 0, 1.0, 0.0).astype(BF16)

    lp = lam_ref[...]
    lam_init = lp[4:5, 0:1]
    lam = (jnp.exp(jnp.sum(lp[0:1] * lp[1:2], axis=-1, keepdims=True))
           - jnp.exp(jnp.sum(lp[2:3] * lp[3:4], axis=-1, keepdims=True)) + lam_init)

    crow = crow_ref[0]
    posq = posq_ref[0]
    row = lax.broadcasted_iota(jnp.int32, (KPAD, TQ), 0)

    def key_block(u, qi):
        return jnp.where(u == 0, qi, u - 1 + jnp.where(u - 1 >= qi, 1, 0))

    def scores(buf, g, m, kj, rhs, diag):
        s = jnp.dot(kaug_ref[kj], rhs, preferred_element_type=F32)
        if diag:
            s = s + dbias_ref[0]
        buf[2 * g + m] = s
        return jnp.max(s, axis=0, keepdims=True)

    def consume(buf, g, m, kj, qi, mt, mrun, acc):
        off = crow * (-lax.convert_element_type(jnp.abs(kj - qi) * TK, F32))
        mnew = jnp.maximum(mrun, mt + off)
        alpha = jnp.exp2(mrun - mnew)
        p = jnp.exp2(buf[2 * g + m] - (mnew - off)).astype(BF16)
        acc = alpha * acc + jnp.dot(vaug_ref[kj], p, preferred_element_type=F32)
        return mnew, acc

    def prologue(qg):
        mts = []
        for g in range(ns):
            qi = qg * ns + g
            qt = qt_ref[0, qi]
            zero = jnp.zeros_like(qt)
            for m in range(2):
                qm = jnp.where((row >= ATT_QK * m) & (row < ATT_QK * (m + 1)), qt, zero)
                rhs_ref[(2 * g + 0) * 2 + m] = qm + posq
                rhs_ref[(2 * g + 1) * 2 + m] = qm - posq
                mts.append(scores(sa_ref, g, m, qi, qm, True))
        return tuple(mts)

    def produce(buf, qg, u):
        out = []
        for g in range(ns):
            qi = qg * ns + g
            kj = key_block(u, qi)
            side = jnp.where(kj > qi, 1, 0)
            for m in range(2):
                out.append(scores(buf, g, m, kj, rhs_ref[(2 * g + side) * 2 + m], False))
        return tuple(out)

    def finish(buf, qg, u, mts, mruns, accs):
        new_mruns, new_accs = [], []
        for g in range(ns):
            qi = qg * ns + g
            kp = key_block(u, qi)
            for m in range(2):
                i = 2 * g + m
                mr, ac = consume(buf, g, m, kp, qi, mts[i], mruns[i], accs[i])
                new_mruns.append(mr)
                new_accs.append(ac)
        return tuple(new_mruns), tuple(new_accs)

    def run_items(qg, first, count, mts, mruns, accs):
        for j in range(count):
            buf, prev = (sb_ref, sa_ref) if j % 2 == 0 else (sa_ref, sb_ref)
            new_mts = produce(buf, qg, first + j)
            mruns, accs = finish(prev, qg, first + j - 1, mts, mruns, accs)
            mts = new_mts
        return mts, mruns, accs

    n_loop = (nk - 1) // ITEMS_PER_TRIP
    n_tail = nk - 1 - n_loop * ITEMS_PER_TRIP

    def qgroup(qg, mts):
        init = (mts,
                tuple(jnp.full((1, TQ), -1e30, F32) for _ in range(2 * ns)),
                tuple(jnp.zeros((V_ROWS, TQ), F32) for _ in range(2 * ns)))

        def trip(t, carry):
            return run_items(qg, ITEMS_PER_TRIP * t + 1, ITEMS_PER_TRIP, *carry)

        carry = lax.fori_loop(0, n_loop, trip, init)
        mts, mruns, accs = run_items(qg, n_loop * ITEMS_PER_TRIP + 1, n_tail, *carry)
        last = sb_ref if (nk - 1) % 2 == 1 else sa_ref
        mruns, accs = finish(last, qg, nk - 1, mts, mruns, accs)
        nxt = prologue(jnp.minimum(qg + 1, nq // ns - 1))
        for g in range(ns):
            outs = [accs[2 * g + m][:ATT_VDIM] / accs[2 * g + m][ATT_VDIM:ATT_VDIM + 1]
                    for m in range(2)]
            o = outs[0] - lam * outs[1]
            ms = jnp.mean(o * o, axis=0, keepdims=True)
            o_ref[0, qg * ns + g] = (o * lax.rsqrt(ms + EPS) * g_ref[...]).astype(BF16)
        return nxt

    lax.fori_loop(0, nq // ns, qgroup, prologue(0))


def _attention(lam_pack, gcol, consts, qt, k, vt):
    b, nq = qt.shape[0], qt.shape[1]
    nk = vt.shape[1]
    s = k.shape[1]
    crow, posk, posq, dbias = consts
    head = lambda shape: pl.BlockSpec((1,) + shape, lambda i, h: (h, 0, 0))
    return pl.pallas_call(
        _attn_kernel,
        out_shape=jax.ShapeDtypeStruct((b, nq, GROUP_WIDTH, TQ), BF16),
        grid=(b, ATT_HEADS),
        in_specs=[
            _full(lam_pack.shape), _full(gcol.shape),
            head((1, TQ)), head((TK, KPAD)), head((KPAD, TQ)), head((TK, TQ)),
            pl.BlockSpec((1, nq, KPAD, TQ), lambda i, h: (i, 0, h, 0)),
            pl.BlockSpec((1, s, KPAD), lambda i, h: (i, 0, h)),
            pl.BlockSpec((1, nk, ATT_VDIM, TK), lambda i, h: (i, 0, h, 0)),
        ],
        out_specs=pl.BlockSpec((1, nq, ATT_VDIM, TQ), lambda i, h: (i, 0, h, 0)),
        scratch_shapes=[pltpu.VMEM((nk, TK, KPAD), BF16), pltpu.VMEM((nk, V_ROWS, TK), BF16),
                        pltpu.VMEM((4 * Q_STREAMS, KPAD, TQ), BF16),
                        pltpu.VMEM((2 * Q_STREAMS, TK, TQ), F32),
                        pltpu.VMEM((2 * Q_STREAMS, TK, TQ), F32)],
        compiler_params=_params(("parallel", "parallel")),
        name="diff_attention",
    )(lam_pack, gcol, crow, posk, posq, dbias, qt, k, vt)


def _attention_consts():
    slopes = np.array([2.0 ** (-8.0 * (i + 1) / ATT_HEADS) for i in range(ATT_HEADS)], np.float64)
    c = (slopes * LOG2E).astype(np.float32)
    parts = []
    rem = c.astype(np.float32)
    for _ in range(N_SPLIT):
        p = rem.astype(BF16).astype(np.float32)
        parts.append(p)
        rem = (rem - p).astype(np.float32)
    rk = np.arange(TK, dtype=np.float32)
    rq = np.arange(TQ, dtype=np.float32)
    posk = np.zeros((ATT_HEADS, TK, KPAD), np.float32)
    posq = np.zeros((ATT_HEADS, KPAD, TQ), np.float32)
    for t in range(N_SPLIT):
        posk[:, :, POS_LANE + t] = rk[None, :]
        posk[:, :, POS_LANE + N_SPLIT + t] = parts[t][:, None]
        posq[:, POS_LANE + t, :] = parts[t][:, None]
        posq[:, POS_LANE + N_SPLIT + t, :] = -rq[None, :]
    dbias = -c[:, None, None] * np.abs(rq[None, None, :] - rk[None, :, None])
    crow = np.broadcast_to(c[:, None, None], (ATT_HEADS, 1, TQ))
    return (jnp.asarray(crow, F32), jnp.asarray(posk, BF16), jnp.asarray(posq, BF16),
            jnp.asarray(dbias, F32))


CONV_PAD = 16
CONV_ROWS = 128


def _split_dot(x, w):
    hi = x.astype(BF16)
    lo = (x - hi.astype(F32)).astype(BF16)
    return (jnp.dot(hi, w, preferred_element_type=F32)
            + jnp.dot(lo, w, preferred_element_type=F32))


def _conv_kernel(hb_ref, dww_ref, dwb_ref, lng_ref, lnb_ref, gavg_ref, pww_ref, pwb_ref,
                 o_ref, z_ref, sh_ref):
    s = hb_ref.shape[1]
    half = CONV_WIDTH // 2
    zeros = jnp.zeros((CONV_PAD, GROUP_WIDTH), F32)
    z_ref[0:CONV_PAD, :] = zeros
    z_ref[CONV_PAD + s:CONV_PAD + s + CONV_PAD, :] = zeros

    def glu(i, _):
        r0 = pl.multiple_of(i * CONV_ROWS, CONV_ROWS)
        h = hb_ref[0, pl.ds(r0, CONV_ROWS), :]
        z_ref[pl.ds(CONV_PAD + r0, CONV_ROWS), :] = h[:, :GROUP_WIDTH] * jax.nn.sigmoid(h[:, GROUP_WIDTH:])
        return 0

    lax.fori_loop(0, s // CONV_ROWS, glu, 0)

    gavg = gavg_ref[...]

    def chunk(i, _):
        r0 = pl.multiple_of(i * CONV_ROWS, CONV_ROWS)
        acc = jnp.zeros((CONV_ROWS, GROUP_WIDTH), F32) + dwb_ref[...]
        win = z_ref[pl.ds(r0, CONV_ROWS + 2 * CONV_PAD), :]
        span = CONV_ROWS + 2 * CONV_PAD - 8
        for j in range(8):
            sh_ref[j] = win[j:j + span]
            for t in range(CONV_WIDTH):
                off = CONV_PAD - half + t
                if off % 8 == j:
                    a = off - j
                    acc = acc + dww_ref[t:t + 1, :] * sh_ref[j, a:a + CONV_ROWS, :]
        mu = _split_dot(acc, gavg)
        xc = acc - mu
        var = _split_dot(xc * xc, gavg)
        y = xc * lax.rsqrt(var + EPS) * lng_ref[...] + lnb_ref[...]
        y = y * jax.nn.sigmoid(y)
        out = jnp.dot(y.astype(BF16), pww_ref[...], preferred_element_type=F32) + pwb_ref[...]
        o_ref[0, pl.ds(r0, CONV_ROWS), :] = out.astype(BF16)
        return 0

    lax.fori_loop(0, s // CONV_ROWS, chunk, 0, unroll=4)


def _conformer_conv(hb, dww, dwb, lng, lnb, gavg, pww, pwb):
    b, s, _ = hb.shape
    return pl.pallas_call(
        _conv_kernel,
        out_shape=jax.ShapeDtypeStruct((b, s, GROUP_WIDTH), BF16),
        grid=(b,),
        in_specs=[pl.BlockSpec((1, s, 2 * GROUP_WIDTH), lambda i: (i, 0, 0)),
                  _full(dww.shape), _full(dwb.shape), _full(lng.shape), _full(lnb.shape),
                  _full(gavg.shape), _full(pww.shape), _full(pwb.shape)],
        out_specs=pl.BlockSpec((1, s, GROUP_WIDTH), lambda i: (i, 0, 0)),
        scratch_shapes=[pltpu.VMEM((s + 2 * CONV_PAD, GROUP_WIDTH), F32),
                        pltpu.VMEM((8, CONV_ROWS + 2 * CONV_PAD - 8, GROUP_WIDTH), F32)],
        compiler_params=_params(("parallel",)),
        name="conformer_conv",
    )(hb, dww, dwb, lng, lnb, gavg, pww, pwb)


FNET_ROWS = 512
FNET_PAD = 8


def _fnet_kernel(c_ref, cmat_ref, m1_ref, m3_ref, twc_ref, tws_ref, w_ref, b_ref,
                 o_ref, z_ref, bb_ref, y_ref):
    s = c_ref.shape[1]
    n1 = m1_ref.shape[0] // 2
    n2 = m3_ref.shape[0]
    gw = GROUP_WIDTH
    rows = min(FNET_ROWS, s)

    nl = gw // 128
    pz, pb, py = n2 + FNET_PAD, 2 * n2 + FNET_PAD, n1 + FNET_PAD

    for i in range(s // rows):
        z = jnp.dot(c_ref[0, i * rows:(i + 1) * rows, :], cmat_ref[...],
                    preferred_element_type=F32)
        for q in range(rows // n2):
            i1 = i * (rows // n2) + q
            for p in range(2 * nl):
                z_ref[p, i1 * pz:i1 * pz + n2, :] = z[q * n2:(q + 1) * n2, p * 128:(p + 1) * 128]

    m1 = m1_ref[...]
    for j in range(n2):
        zcat = jnp.concatenate(
            [jnp.concatenate([z_ref[comp * nl + p, pl.ds(j, n1, stride=pz), :] for p in range(nl)],
                             axis=1) for comp in range(2)], axis=0)
        a = jnp.dot(m1, zcat.astype(BF16), preferred_element_type=F32)
        tc = twc_ref[j]
        ts = tws_ref[j]
        for p in range(nl):
            ar = a[:n1, p * 128:(p + 1) * 128]
            ai = a[n1:, p * 128:(p + 1) * 128]
            bb_ref[p, pl.ds(j, n1, stride=pb), :] = ar * tc + ai * ts
            bb_ref[p, pl.ds(n2 + j, n1, stride=pb), :] = ai * tc - ar * ts

    m3 = m3_ref[...]
    for k1 in range(n1):
        rhs = jnp.concatenate([bb_ref[p, k1 * pb:k1 * pb + 2 * n2, :] for p in range(nl)],
                              axis=1).astype(BF16)
        y = jnp.dot(m3, rhs, preferred_element_type=F32)
        for p in range(nl):
            y_ref[p, pl.ds(k1, n2, stride=py), :] = y[:, p * 128:(p + 1) * 128]

    norm = 1.0 / math.sqrt(s * (gw // FNET_GROUPS))
    for i in range(s // rows):
        y = jnp.concatenate(
            [jnp.concatenate([y_ref[p, k2 * py:k2 * py + n1, :] for p in range(nl)], axis=1)
             for k2 in range(i * (rows // n1), (i + 1) * (rows // n1))], axis=0)
        out = jnp.dot((y * norm).astype(BF16), w_ref[...], preferred_element_type=F32) + b_ref[...]
        o_ref[0, i * rows:(i + 1) * rows, :] = out.astype(BF16)


def _fnet_consts(s):
    n1 = 1 << (int(math.log2(s)) // 2)
    n2 = s // n1
    cw = GROUP_WIDTH // FNET_GROUPS
    ang = 2.0 * np.pi * np.outer(np.arange(cw), np.arange(cw)) / cw
    eye = np.eye(FNET_GROUPS)
    cmat = np.concatenate([np.kron(eye, np.cos(ang)), -np.kron(eye, np.sin(ang))], axis=1)
    a1 = 2.0 * np.pi * np.outer(np.arange(n1), np.arange(n1)) / n1
    m1 = np.block([[np.cos(a1), np.sin(a1)], [-np.sin(a1), np.cos(a1)]])
    a2 = 2.0 * np.pi * np.outer(np.arange(n2), np.arange(n2)) / n2
    m3 = np.concatenate([np.cos(a2), np.sin(a2)], axis=1)
    at = 2.0 * np.pi * np.outer(np.arange(n2), np.arange(n1)) / s
    twc = np.broadcast_to(np.cos(at)[:, :, None], (n2, n1, 128))
    tws = np.broadcast_to(np.sin(at)[:, :, None], (n2, n1, 128))
    return (jnp.asarray(cmat, BF16), jnp.asarray(m1, BF16), jnp.asarray(m3, BF16),
            jnp.asarray(twc, F32), jnp.asarray(tws, F32))


def _fourier_mix(c, consts, w_bd, bias):
    b, s, gw = c.shape
    cmat, m1, m3, twc, tws = consts
    n1, n2 = m1.shape[0] // 2, m3.shape[0]
    return pl.pallas_call(
        _fnet_kernel,
        out_shape=jax.ShapeDtypeStruct((b, s, gw), BF16),
        grid=(b,),
        in_specs=[pl.BlockSpec((1, s, gw), lambda i: (i, 0, 0)),
                  _full(cmat.shape), _full(m1.shape), _full(m3.shape),
                  _full(twc.shape), _full(tws.shape), _full(w_bd.shape), _full(bias.shape)],
        out_specs=pl.BlockSpec((1, s, gw), lambda i: (i, 0, 0)),
        scratch_shapes=[pltpu.VMEM((2 * gw // 128, n1 * (n2 + FNET_PAD), 128), F32),
                        pltpu.VMEM((gw // 128, n1 * (2 * n2 + FNET_PAD), 128), F32),
                        pltpu.VMEM((gw // 128, n2 * (n1 + FNET_PAD), 128), F32)],
        compiler_params=_params(("parallel",)),
        name="fourier_mix",
    )(c, cmat, m1, m3, twc, tws, w_bd, bias)


def _outproj_kernel(x_ref, ya_ref, yb_ref, yc_ref, yd_ref, w_ref, o_ref):
    acc = (x_ref[0]
           + jnp.dot(yb_ref[0], w_ref[1], preferred_element_type=F32)
           + jnp.dot(yc_ref[0], w_ref[2], preferred_element_type=F32)
           + jnp.dot(yd_ref[0], w_ref[3], preferred_element_type=F32))
    for j in range(TM // TQ):
        ya = lax.dot_general(ya_ref[0, j], w_ref[0], (((0,), (0,)), ((), ())),
                             preferred_element_type=F32)
        o_ref[0, j * TQ:(j + 1) * TQ, :] = acc[j * TQ:(j + 1) * TQ] + ya


def _outproj(x, ya_t, yb, yc, yd, w):
    b, s, d = x.shape
    gw = GROUP_WIDTH
    tok = lambda width: pl.BlockSpec((1, TM, width), lambda i, j: (i, j, 0))
    return pl.pallas_call(
        _outproj_kernel,
        out_shape=jax.ShapeDtypeStruct((b, s, d), F32),
        grid=(b, s // TM),
        in_specs=[tok(d), pl.BlockSpec((1, TM // TQ, gw, TQ), lambda i, j: (i, j, 0, 0)),
                  tok(gw), tok(gw), tok(gw), _full(w.shape)],
        out_specs=tok(d),
        compiler_params=_params(("parallel", "parallel")),
        name="outproj",
    )(x, ya_t, yb, yc, yd, w)


FF_CHUNK = 1024


def _mlp_kernel(x_ref, g_ref, wu_ref, wd_ref, gf_ref, o_ref, *, final):
    x = x_ref[0]
    ms = jnp.mean(x * x, axis=-1, keepdims=True)
    xn = (x * lax.rsqrt(ms + EPS) * g_ref[...]).astype(BF16)
    acc = x
    for c in range(D_FF // FF_CHUNK):
        cs = slice(c * FF_CHUNK, (c + 1) * FF_CHUNK)
        h = jnp.maximum(jnp.dot(xn, wu_ref[:, cs], preferred_element_type=F32), 0.0)
        acc = acc + jnp.dot((h * h).astype(BF16), wd_ref[cs, :], preferred_element_type=F32)
    if final:
        ms = jnp.mean(acc * acc, axis=-1, keepdims=True)
        acc = acc * lax.rsqrt(ms + EPS) * gf_ref[...]
    o_ref[0] = acc


def _mlp(x, g, wu, wd, gf, final):
    b, s, d = x.shape
    tok = pl.BlockSpec((1, TM, d), lambda i, j: (i, j, 0))
    const = lambda shape: pl.BlockSpec(shape, lambda i, j: (0, 0), pipeline_mode=pl.Buffered(1))
    return pl.pallas_call(
        functools.partial(_mlp_kernel, final=final),
        out_shape=jax.ShapeDtypeStruct((b, s, d), F32),
        grid=(b, s // TM),
        in_specs=[tok, _full(g.shape), const(wu.shape), const(wd.shape), _full(gf.shape)],
        out_specs=tok,
        compiler_params=_params(("parallel", "parallel")),
        name="mlp",
    )(x, g, wu, wd, gf)


def _block_diag(blocks):
    g, r, c = blocks.shape
    eye = jnp.eye(g, dtype=blocks.dtype)
    return (eye[:, None, :, None] * blocks[:, :, None, :]).reshape(g * r, g * c)


def _layer_weights(w_in_l):
    gw = GROUP_WIDTH
    wn = w_in_l[:, gw:2 * gw]
    wn = jnp.concatenate([wn, w_in_l[:, 3 * gw:]], axis=1).astype(BF16)
    wt = jnp.concatenate([w_in_l[:, :gw], w_in_l[:, 2 * gw:3 * gw]], axis=1).T.astype(BF16)
    return wn, wt


def kernel(x, norm1_g, w_in, lam_q1, lam_k1, lam_q2, lam_k2, subln_g, conv_dw_w, conv_dw_b, conv_ln_g, conv_ln_b, conv_pw_w, conv_pw_b, fnet_w, fnet_b, sgu_ln_g, sgu_ln_b, sgu_w, sgu_b, w_out, norm2_g, w_up, w_down, final_g):
    bsz, seq, d = x.shape
    depth = w_in.shape[0]
    gw = GROUP_WIDTH
    assert seq % TM == 0 and d == D_MODEL

    attn_consts = _attention_consts()
    fnet_consts = _fnet_consts(seq)
    cg = gw // CONV_GROUPS
    gavg = jnp.asarray(np.kron(np.eye(CONV_GROUPS), np.full((cg, cg), 1.0 / cg)), BF16)
    row = lambda v: v.reshape(1, -1).astype(F32)

    for l in range(depth):
        wn, wt = _layer_weights(w_in[l])
        wcat = sgu_w[l].reshape(SGU_GROUPS * SGU_CHUNK, SGU_CHUNK).astype(BF16)
        bias_plane = jnp.repeat(sgu_b[l].T.astype(F32), gw // SGU_GROUPS, axis=1)
        k, hb, hc, yd, qt, vt = _inproj(x, row(norm1_g[l]), wn, wt, row(sgu_ln_g[l]),
                                        row(sgu_ln_b[l]), wcat, bias_plane)

        lam_init = 0.8 - 0.6 * math.exp(-0.3 * l)
        lam_pack = jnp.concatenate(
            [lam_q1[l][None], lam_k1[l][None], lam_q2[l][None], lam_k2[l][None],
             jnp.full((4, ATT_QK), lam_init, F32)], axis=0).astype(F32)
        gcol = (subln_g[l].astype(F32) * (1.0 - lam_init)).reshape(ATT_VDIM, 1)
        ya_t = _attention(lam_pack, gcol, attn_consts, qt, k, vt)

        yb = _conformer_conv(hb, conv_dw_w[l].astype(F32), row(conv_dw_b[l]), row(conv_ln_g[l]),
                             row(conv_ln_b[l]), gavg, conv_pw_w[l].astype(BF16), row(conv_pw_b[l]))

        yc = _fourier_mix(hc, fnet_consts, _block_diag(fnet_w[l]).astype(BF16), row(fnet_b[l]))

        x = _outproj(x, ya_t, yb, yc, yd, w_out[l].reshape(4, gw, d).astype(BF16))
        x = _mlp(x, row(norm2_g[l]), w_up[l].astype(BF16), w_down[l].astype(BF16),
                 row(final_g), l == depth - 1)

    return x
```

```python
import functools
import math

import numpy as np
import jax
import jax.numpy as jnp
from jax import lax
from jax.experimental import pallas as pl
from jax.experimental.pallas import tpu as pltpu

D_MODEL = 1024
GROUP_WIDTH = 256
ATT_HEADS = 4
ATT_VDIM = 64
ATT_QK = 32
CONV_WIDTH = 31
CONV_GROUPS = 4
FNET_GROUPS = 4
SGU_CHUNK = 128
SGU_GROUPS = 4
D_FF = 4 * D_MODEL
EPS = 1e-6
LOG2E = 1.4426950408889634

KPAD = 128
POS_LANE = 2 * ATT_QK
N_SPLIT = 3
TQ = 256
TK = 256
V_ROWS = ATT_VDIM + 16
Q_STREAMS = 2
ITEMS_PER_TRIP = 16
TM = 512
VMEM_LIMIT = 56 * 1024 * 1024

BF16 = jnp.bfloat16
F32 = jnp.float32


def _params(sem, vmem=VMEM_LIMIT):
    return pltpu.CompilerParams(dimension_semantics=sem, vmem_limit_bytes=vmem)


def _full(shape):
    n = len(shape)
    return pl.BlockSpec(shape, lambda *_: (0,) * n)


def _sgu(u, v, g_ref, b_ref, w_ref, bias_ref, o_ref):
    mu = jnp.mean(v, axis=-1, keepdims=True)
    xc = v - mu
    var = jnp.mean(xc * xc, axis=-1, keepdims=True)
    vn = (xc * lax.rsqrt(var + EPS) * g_ref[...] + b_ref[...]).astype(BF16)
    cg = GROUP_WIDTH // SGU_GROUPS
    lane = lax.broadcasted_iota(jnp.int32, (SGU_CHUNK, GROUP_WIDTH), 1)
    w = w_ref[...]
    for c in range(TM // SGU_CHUNK):
        rs = slice(c * SGU_CHUNK, (c + 1) * SGU_CHUNK)
        r = jnp.dot(w, vn[rs], preferred_element_type=F32)
        sv = r[(SGU_GROUPS - 1) * SGU_CHUNK:]
        for g in range(SGU_GROUPS - 2, -1, -1):
            sv = jnp.where(lane < (g + 1) * cg, r[g * SGU_CHUNK:(g + 1) * SGU_CHUNK], sv)
        o_ref[0, rs, :] = (u[rs] * (sv + bias_ref[...])).astype(BF16)


def _inproj_kernel(x_ref, g_ref, wn_ref, wt_ref, sg_ref, sb_ref, sw_ref, sbias_ref,
                   k_ref, hb_ref, hc_ref, yd_ref, qt_ref, vt_ref):
    gw = GROUP_WIDTH
    x = x_ref[0]
    ms = jnp.mean(x * x, axis=-1, keepdims=True)
    xn = (x * lax.rsqrt(ms + EPS) * g_ref[...]).astype(BF16)
    nat = jnp.dot(xn, wn_ref[...], preferred_element_type=F32)
    tr = lax.dot_general(wt_ref[...], xn, (((1,), (1,)), ((), ())),
                         preferred_element_type=F32)
    hw = 2 * ATT_QK
    kz = jnp.zeros((TM, KPAD - hw), F32)
    k_ref[0] = jnp.concatenate(
        [piece for h in range(ATT_HEADS) for piece in (nat[:, h * hw:(h + 1) * hw], kz)],
        axis=1).astype(BF16)
    hb_ref[0] = nat[:, gw:3 * gw]
    hc_ref[0] = nat[:, 3 * gw:4 * gw].astype(BF16)
    _sgu(nat[:, 4 * gw:5 * gw], nat[:, 5 * gw:6 * gw], sg_ref, sb_ref, sw_ref, sbias_ref, yd_ref)
    qscale = (ATT_QK ** -0.5) * LOG2E
    qz = jnp.zeros((KPAD - hw, TM), F32)
    q = jnp.concatenate(
        [piece for h in range(ATT_HEADS) for piece in (tr[h * hw:(h + 1) * hw] * qscale, qz)],
        axis=0).astype(BF16)
    for j in range(TM // TQ):
        qt_ref[0, j] = q[:, j * TQ:(j + 1) * TQ]
    for j in range(TM // TK):
        vt_ref[0, j] = tr[gw:, j * TK:(j + 1) * TK].astype(BF16)


def _inproj(x, g, wn, wt, sgu_g, sgu_b, sgu_w, sgu_bias):
    b, s, d = x.shape
    kw = ATT_HEADS * KPAD
    gw = GROUP_WIDTH
    grid = (b, s // TM)
    tok = lambda w: pl.BlockSpec((1, TM, w), lambda i, j: (i, j, 0))
    out_shape = (
        jax.ShapeDtypeStruct((b, s, kw), BF16),
        jax.ShapeDtypeStruct((b, s, 2 * gw), F32),
        jax.ShapeDtypeStruct((b, s, gw), BF16),
        jax.ShapeDtypeStruct((b, s, gw), BF16),
        jax.ShapeDtypeStruct((b, s // TQ, kw, TQ), BF16),
        jax.ShapeDtypeStruct((b, s // TK, gw, TK), BF16),
    )
    out_specs = (
        tok(kw), tok(2 * gw), tok(gw), tok(gw),
        pl.BlockSpec((1, TM // TQ, kw, TQ), lambda i, j: (i, j, 0, 0)),
        pl.BlockSpec((1, TM // TK, gw, TK), lambda i, j: (i, j, 0, 0)),
    )
    consts = (g, wn, wt, sgu_g, sgu_b, sgu_w, sgu_bias)
    return pl.pallas_call(
        _inproj_kernel,
        out_shape=out_shape,
        grid=grid,
        in_specs=[tok(d)] + [_full(c.shape) for c in consts],
        out_specs=out_specs,
        compiler_params=_params(("parallel", "parallel")),
        name="inproj",
    )(x, *consts)


def _attn_kernel(lam_ref, g_ref, crow_ref, posk_ref, posq_ref, dbias_ref,
                 qt_ref, k_ref, vt_ref, o_ref, kaug_ref, vaug_ref, rhs_ref, sa_ref, sb_ref):
    nq = qt_ref.shape[1]
    nk = vt_ref.shape[1]
    ns = Q_STREAMS

    for j in range(nk):
        kaug_ref[j] = k_ref[0, j * TK:(j + 1) * TK, :] + posk_ref[0]
        vaug_ref[j, :ATT_VDIM, :] = vt_ref[0, j]
        r = lax.broadcasted_iota(jnp.int32, (V_ROWS - ATT_VDIM, TK), 0)
        vaug_ref[j, ATT_VDIM:, :] = jnp.where(r == 0, 1.0, 0.0).astype(BF16)

    lp = lam_ref[...]
    lam_init = lp[4:5, 0:1]
    lam = (jnp.exp(jnp.sum(lp[0:1] * lp[1:2], axis=-1, keepdims=True))
           - jnp.exp(jnp.sum(lp[2:3] * lp[3:4], axis=-1, keepdims=True)) + lam_init)

    crow = crow_ref[0]
    posq = posq_ref[0]
    row = lax.broadcasted_iota(jnp.int32, (KPAD, TQ), 0)

    def key_block(u, qi):
        return jnp.where(u == 0, qi, u - 1 + jnp.where(u - 1 >= qi, 1, 0))

    def scores(buf, g, m, kj, rhs, diag):
        s = jnp.dot(kaug_ref[kj], rhs, preferred_element_type=F32)
        if diag:
            s = s + dbias_ref[0]
        buf[2 * g + m] = s
        return jnp.max(s, axis=0, keepdims=True)

    def consume(buf, g, m, kj, qi, mt, mrun, acc):
        off = crow * (-lax.convert_element_type(jnp.abs(kj - qi) * TK, F32))
        mnew = jnp.maximum(mrun, mt + off)
        alpha = jnp.exp2(mrun - mnew)
        p = jnp.exp2(buf[2 * g + m] - (mnew - off)).astype(BF16)
        acc = alpha * acc + jnp.dot(vaug_ref[kj], p, preferred_element_type=F32)
        return mnew, acc

    def prologue(qg):
        mts = []
        for g in range(ns):
            qi = qg * ns + g
            qt = qt_ref[0, qi]
            zero = jnp.zeros_like(qt)
            for m in range(2):
                qm = jnp.where((row >= ATT_QK * m) & (row < ATT_QK * (m + 1)), qt, zero)
                rhs_ref[(2 * g + 0) * 2 + m] = qm + posq
                rhs_ref[(2 * g + 1) * 2 + m] = qm - posq
                mts.append(scores(sa_ref, g, m, qi, qm, True))
        return tuple(mts)

    def produce(buf, qg, u):
        out = []
        for g in range(ns):
            qi = qg * ns + g
            kj = key_block(u, qi)
            side = jnp.where(kj > qi, 1, 0)
            for m in range(2):
                out.append(scores(buf, g, m, kj, rhs_ref[(2 * g + side) * 2 + m], False))
        return tuple(out)

    def finish(buf, qg, u, mts, mruns, accs):
        new_mruns, new_accs = [], []
        for g in range(ns):
            qi = qg * ns + g
            kp = key_block(u, qi)
            for m in range(2):
                i = 2 * g + m
                mr, ac = consume(buf, g, m, kp, qi, mts[i], mruns[i], accs[i])
                new_mruns.append(mr)
                new_accs.append(ac)
        return tuple(new_mruns), tuple(new_accs)

    def run_items(qg, first, count, mts, mruns, accs):
        for j in range(count):
            buf, prev = (sb_ref, sa_ref) if j % 2 == 0 else (sa_ref, sb_ref)
            new_mts = produce(buf, qg, first + j)
            mruns, accs = finish(prev, qg, first + j - 1, mts, mruns, accs)
            mts = new_mts
        return mts, mruns, accs

    n_loop = (nk - 1) // ITEMS_PER_TRIP
    n_tail = nk - 1 - n_loop * ITEMS_PER_TRIP

    def qgroup(qg, mts):
        init = (mts,
                tuple(jnp.full((1, TQ), -1e30, F32) for _ in range(2 * ns)),
                tuple(jnp.zeros((V_ROWS, TQ), F32) for _ in range(2 * ns)))

        def trip(t, carry):
            return run_items(qg, ITEMS_PER_TRIP * t + 1, ITEMS_PER_TRIP, *carry)

        carry = lax.fori_loop(0, n_loop, trip, init)
        mts, mruns, accs = run_items(qg, n_loop * ITEMS_PER_TRIP + 1, n_tail, *carry)
        last = sb_ref if (nk - 1) % 2 == 1 else sa_ref
        mruns, accs = finish(last, qg, nk - 1, mts, mruns, accs)
        nxt = prologue(jnp.minimum(qg + 1, nq // ns - 1))
        for g in range(ns):
            outs = [accs[2 * g + m][:ATT_VDIM] / accs[2 * g + m][ATT_VDIM:ATT_VDIM + 1]
                    for m in range(2)]
            o = outs[0] - lam * outs[1]
            ms = jnp.mean(o * o, axis=0, keepdims=True)
            o_ref[0, qg * ns + g] = (o * lax.rsqrt(ms + EPS) * g_ref[...]).astype(BF16)
        return nxt

    lax.fori_loop(0, nq // ns, qgroup, prologue(0))


def _attention(lam_pack, gcol, consts, qt, k, vt):
    b, nq = qt.shape[0], qt.shape[1]
    nk = vt.shape[1]
    s = k.shape[1]
    crow, posk, posq, dbias = consts
    head = lambda shape: pl.BlockSpec((1,) + shape, lambda i, h: (h, 0, 0))
    return pl.pallas_call(
        _attn_kernel,
        out_shape=jax.ShapeDtypeStruct((b, nq, GROUP_WIDTH, TQ), BF16),
        grid=(b, ATT_HEADS),
        in_specs=[
            _full(lam_pack.shape), _full(gcol.shape),
            head((1, TQ)), head((TK, KPAD)), head((KPAD, TQ)), head((TK, TQ)),
            pl.BlockSpec((1, nq, KPAD, TQ), lambda i, h: (i, 0, h, 0)),
            pl.BlockSpec((1, s, KPAD), lambda i, h: (i, 0, h)),
            pl.BlockSpec((1, nk, ATT_VDIM, TK), lambda i, h: (i, 0, h, 0)),
        ],
        out_specs=pl.BlockSpec((1, nq, ATT_VDIM, TQ), lambda i, h: (i, 0, h, 0)),
        scratch_shapes=[pltpu.VMEM((nk, TK, KPAD), BF16), pltpu.VMEM((nk, V_ROWS, TK), BF16),
                        pltpu.VMEM((4 * Q_STREAMS, KPAD, TQ), BF16),
                        pltpu.VMEM((2 * Q_STREAMS, TK, TQ), F32),
                        pltpu.VMEM((2 * Q_STREAMS, TK, TQ), F32)],
        compiler_params=_params(("parallel", "parallel")),
        name="diff_attention",
    )(lam_pack, gcol, crow, posk, posq, dbias, qt, k, vt)


def _attention_consts():
    slopes = np.array([2.0 ** (-8.0 * (i + 1) / ATT_HEADS) for i in range(ATT_HEADS)], np.float64)
    c = (slopes * LOG2E).astype(np.float32)
    parts = []
    rem = c.astype(np.float32)
    for _ in range(N_SPLIT):
        p = rem.astype(BF16).astype(np.float32)
        parts.append(p)
        rem = (rem - p).astype(np.float32)
    rk = np.arange(TK, dtype=np.float32)
    rq = np.arange(TQ, dtype=np.float32)
    posk = np.zeros((ATT_HEADS, TK, KPAD), np.float32)
    posq = np.zeros((ATT_HEADS, KPAD, TQ), np.float32)
    for t in range(N_SPLIT):
        posk[:, :, POS_LANE + t] = rk[None, :]
        posk[:, :, POS_LANE + N_SPLIT + t] = parts[t][:, None]
        posq[:, POS_LANE + t, :] = parts[t][:, None]
        posq[:, POS_LANE + N_SPLIT + t, :] = -rq[None, :]
    dbias = -c[:, None, None] * np.abs(rq[None, None, :] - rk[None, :, None])
    crow = np.broadcast_to(c[:, None, None], (ATT_HEADS, 1, TQ))
    return (jnp.asarray(crow, F32), jnp.asarray(posk, BF16), jnp.asarray(posq, BF16),
            jnp.asarray(dbias, F32))


CONV_PAD = 16
CONV_ROWS = 256


def _split_dot(x, w):
    hi = x.astype(BF16)
    lo = (x - hi.astype(F32)).astype(BF16)
    return (jnp.dot(hi, w, preferred_element_type=F32)
            + jnp.dot(lo, w, preferred_element_type=F32))


def _conv_kernel(hb_ref, dww_ref, dwb_ref, lng_ref, lnb_ref, gavg_ref, pww_ref, pwb_ref,
                 o_ref, z_ref, sh_ref):
    s = hb_ref.shape[1]
    half = CONV_WIDTH // 2
    zeros = jnp.zeros((CONV_PAD, GROUP_WIDTH), F32)
    z_ref[0:CONV_PAD, :] = zeros
    z_ref[CONV_PAD + s:CONV_PAD + s + CONV_PAD, :] = zeros

    def glu(i, _):
        r0 = pl.multiple_of(i * CONV_ROWS, CONV_ROWS)
        h = hb_ref[0, pl.ds(r0, CONV_ROWS), :]
        z_ref[pl.ds(CONV_PAD + r0, CONV_ROWS), :] = h[:, :GROUP_WIDTH] * jax.nn.sigmoid(h[:, GROUP_WIDTH:])
        return 0

    lax.fori_loop(0, s // CONV_ROWS, glu, 0)

    gavg = gavg_ref[...]

    def chunk(i, _):
        r0 = pl.multiple_of(i * CONV_ROWS, CONV_ROWS)
        acc = jnp.zeros((CONV_ROWS, GROUP_WIDTH), F32) + dwb_ref[...]
        win = z_ref[pl.ds(r0, CONV_ROWS + 2 * CONV_PAD), :]
        span = CONV_ROWS + 2 * CONV_PAD - 8
        for j in range(8):
            sh_ref[j] = win[j:j + span]
            for t in range(CONV_WIDTH):
                off = CONV_PAD - half + t
                if off % 8 == j:
                    a = off - j
                    acc = acc + dww_ref[t:t + 1, :] * sh_ref[j, a:a + CONV_ROWS, :]
        mu = _split_dot(acc, gavg)
        xc = acc - mu
        var = _split_dot(xc * xc, gavg)
        y = xc * lax.rsqrt(var + EPS) * lng_ref[...] + lnb_ref[...]
        y = y * jax.nn.sigmoid(y)
        out = jnp.dot(y.astype(BF16), pww_ref[...], preferred_element_type=F32) + pwb_ref[...]
        o_ref[0, pl.ds(r0, CONV_ROWS), :] = out.astype(BF16)
        return 0

    lax.fori_loop(0, s // CONV_ROWS, chunk, 0, unroll=2)


def _conformer_conv(hb, dww, dwb, lng, lnb, gavg, pww, pwb):
    b, s, _ = hb.shape
    return pl.pallas_call(
        _conv_kernel,
        out_shape=jax.ShapeDtypeStruct((b, s, GROUP_WIDTH), BF16),
        grid=(b,),
        in_specs=[pl.BlockSpec((1, s, 2 * GROUP_WIDTH), lambda i: (i, 0, 0)),
                  _full(dww.shape), _full(dwb.shape), _full(lng.shape), _full(lnb.shape),
                  _full(gavg.shape), _full(pww.shape), _full(pwb.shape)],
        out_specs=pl.BlockSpec((1, s, GROUP_WIDTH), lambda i: (i, 0, 0)),
        scratch_shapes=[pltpu.VMEM((s + 2 * CONV_PAD, GROUP_WIDTH), F32),
                        pltpu.VMEM((8, CONV_ROWS + 2 * CONV_PAD - 8, GROUP_WIDTH), F32)],
        compiler_params=_params(("parallel",)),
        name="conformer_conv",
    )(hb, dww, dwb, lng, lnb, gavg, pww, pwb)


FNET_ROWS = 512
FNET_PAD = 8


def _fnet_kernel(c_ref, cmat_ref, m1_ref, m3_ref, twc_ref, tws_ref, w_ref, b_ref,
                 o_ref, z_ref, bb_ref, y_ref):
    s = c_ref.shape[1]
    n1 = m1_ref.shape[0] // 2
    n2 = m3_ref.shape[0]
    gw = GROUP_WIDTH
    rows = min(FNET_ROWS, s)

    nl = gw // 128
    pz, pb, py = n2 + FNET_PAD, 2 * n2 + FNET_PAD, n1 + FNET_PAD

    for i in range(s // rows):
        z = jnp.dot(c_ref[0, i * rows:(i + 1) * rows, :], cmat_ref[...],
                    preferred_element_type=F32)
        for q in range(rows // n2):
            i1 = i * (rows // n2) + q
            for p in range(2 * nl):
                z_ref[p, i1 * pz:i1 * pz + n2, :] = z[q * n2:(q + 1) * n2, p * 128:(p + 1) * 128]

    m1 = m1_ref[...]
    for j in range(n2):
        zcat = jnp.concatenate(
            [jnp.concatenate([z_ref[comp * nl + p, pl.ds(j, n1, stride=pz), :] for p in range(nl)],
                             axis=1) for comp in range(2)], axis=0)
        a = jnp.dot(m1, zcat.astype(BF16), preferred_element_type=F32)
        tc = twc_ref[j]
        ts = tws_ref[j]
        for p in range(nl):
            ar = a[:n1, p * 128:(p + 1) * 128]
            ai = a[n1:, p * 128:(p + 1) * 128]
            bb_ref[p, pl.ds(j, n1, stride=pb), :] = ar * tc + ai * ts
            bb_ref[p, pl.ds(n2 + j, n1, stride=pb), :] = ai * tc - ar * ts

    m3 = m3_ref[...]
    for k1 in range(n1):
        rhs = jnp.concatenate([bb_ref[p, k1 * pb:k1 * pb + 2 * n2, :] for p in range(nl)],
                              axis=1).astype(BF16)
        y = jnp.dot(m3, rhs, preferred_element_type=F32)
        for p in range(nl):
            y_ref[p, pl.ds(k1, n2, stride=py), :] = y[:, p * 128:(p + 1) * 128]

    norm = 1.0 / math.sqrt(s * (gw // FNET_GROUPS))
    for i in range(s // rows):
        y = jnp.concatenate(
            [jnp.concatenate([y_ref[p, k2 * py:k2 * py + n1, :] for p in range(nl)], axis=1)
             for k2 in range(i * (rows // n1), (i + 1) * (rows // n1))], axis=0)
        out = jnp.dot((y * norm).astype(BF16), w_ref[...], preferred_element_type=F32) + b_ref[...]
        o_ref[0, i * rows:(i + 1) * rows, :] = out.astype(BF16)


def _fnet_consts(s):
    n1 = 1 << (int(math.log2(s)) // 2)
    n2 = s // n1
    cw = GROUP_WIDTH // FNET_GROUPS
    ang = 2.0 * np.pi * np.outer(np.arange(cw), np.arange(cw)) / cw
    eye = np.eye(FNET_GROUPS)
    cmat = np.concatenate([np.kron(eye, np.cos(ang)), -np.kron(eye, np.sin(ang))], axis=1)
    a1 = 2.0 * np.pi * np.outer(np.arange(n1), np.arange(n1)) / n1
    m1 = np.block([[np.cos(a1), np.sin(a1)], [-np.sin(a1), np.cos(a1)]])
    a2 = 2.0 * np.pi * np.outer(np.arange(n2), np.arange(n2)) / n2
    m3 = np.concatenate([np.cos(a2), np.sin(a2)], axis=1)
    at = 2.0 * np.pi * np.outer(np.arange(n2), np.arange(n1)) / s
    twc = np.broadcast_to(np.cos(at)[:, :, None], (n2, n1, 128))
    tws = np.broadcast_to(np.sin(at)[:, :, None], (n2, n1, 128))
    return (jnp.asarray(cmat, BF16), jnp.asarray(m1, BF16), jnp.asarray(m3, BF16),
            jnp.asarray(twc, F32), jnp.asarray(tws, F32))


def _fourier_mix(c, consts, w_bd, bias):
    b, s, gw = c.shape
    cmat, m1, m3, twc, tws = consts
    n1, n2 = m1.shape[0] // 2, m3.shape[0]
    return pl.pallas_call(
        _fnet_kernel,
        out_shape=jax.ShapeDtypeStruct((b, s, gw), BF16),
        grid=(b,),
        in_specs=[pl.BlockSpec((1, s, gw), lambda i: (i, 0, 0)),
                  _full(cmat.shape), _full(m1.shape), _full(m3.shape),
                  _full(twc.shape), _full(tws.shape), _full(w_bd.shape), _full(bias.shape)],
        out_specs=pl.BlockSpec((1, s, gw), lambda i: (i, 0, 0)),
        scratch_shapes=[pltpu.VMEM((2 * gw // 128, n1 * (n2 + FNET_PAD), 128), F32),
                        pltpu.VMEM((gw // 128, n1 * (2 * n2 + FNET_PAD), 128), F32),
                        pltpu.VMEM((gw // 128, n2 * (n1 + FNET_PAD), 128), F32)],
        compiler_params=_params(("parallel",)),
        name="fourier_mix",
    )(c, cmat, m1, m3, twc, tws, w_bd, bias)


FF_CHUNK = 1024


def _channel_kernel(x_ref, ya_ref, yb_ref, yc_ref, yd_ref, wo_ref, g_ref, wu_ref, wd_ref, gf_ref,
                    o_ref, xm_ref, *, final):
    phase = pl.program_id(2)

    @pl.when(phase == 0)
    def _():
        acc = (x_ref[0]
               + jnp.dot(yb_ref[0], wo_ref[1], preferred_element_type=F32)
               + jnp.dot(yc_ref[0], wo_ref[2], preferred_element_type=F32)
               + jnp.dot(yd_ref[0], wo_ref[3], preferred_element_type=F32))
        for j in range(TM // TQ):
            ya = lax.dot_general(ya_ref[0, j], wo_ref[0], (((0,), (0,)), ((), ())),
                                 preferred_element_type=F32)
            xm_ref[j * TQ:(j + 1) * TQ, :] = acc[j * TQ:(j + 1) * TQ] + ya

    @pl.when(phase == 1)
    def _():
        x = xm_ref[...]
        ms = jnp.mean(x * x, axis=-1, keepdims=True)
        xn = (x * lax.rsqrt(ms + EPS) * g_ref[...]).astype(BF16)
        acc = x
        for c in range(D_FF // FF_CHUNK):
            cs = slice(c * FF_CHUNK, (c + 1) * FF_CHUNK)
            h = jnp.maximum(jnp.dot(xn, wu_ref[:, cs], preferred_element_type=F32), 0.0)
            acc = acc + jnp.dot((h * h).astype(BF16), wd_ref[cs, :], preferred_element_type=F32)
        if final:
            ms = jnp.mean(acc * acc, axis=-1, keepdims=True)
            acc = acc * lax.rsqrt(ms + EPS) * gf_ref[...]
        o_ref[0] = acc


def _channel_mix(x, ya_t, yb, yc, yd, wo, g, wu, wd, gf, final):
    b, s, d = x.shape
    gw = GROUP_WIDTH
    tok = lambda width: pl.BlockSpec((1, TM, width), lambda i, j, p: (i, j, 0))
    const = lambda shape: pl.BlockSpec(shape, lambda i, j, p: (0,) * len(shape),
                                       pipeline_mode=pl.Buffered(1))
    return pl.pallas_call(
        functools.partial(_channel_kernel, final=final),
        out_shape=jax.ShapeDtypeStruct((b, s, d), F32),
        grid=(b, s // TM, 2),
        in_specs=[tok(d), pl.BlockSpec((1, TM // TQ, gw, TQ), lambda i, j, p: (i, j, 0, 0)),
                  tok(gw), tok(gw), tok(gw), const(wo.shape), const(g.shape),
                  const(wu.shape), const(wd.shape), const(gf.shape)],
        out_specs=tok(d),
        scratch_shapes=[pltpu.VMEM((TM, d), F32)],
        compiler_params=_params(("parallel", "parallel", "arbitrary")),
        name="channel_mix",
    )(x, ya_t, yb, yc, yd, wo, g, wu, wd, gf)


def _block_diag(blocks):
    g, r, c = blocks.shape
    eye = jnp.eye(g, dtype=blocks.dtype)
    return (eye[:, None, :, None] * blocks[:, :, None, :]).reshape(g * r, g * c)


def _layer_weights(w_in_l):
    gw = GROUP_WIDTH
    wn = w_in_l[:, gw:2 * gw]
    wn = jnp.concatenate([wn, w_in_l[:, 3 * gw:]], axis=1).astype(BF16)
    wt = jnp.concatenate([w_in_l[:, :gw], w_in_l[:, 2 * gw:3 * gw]], axis=1).T.astype(BF16)
    return wn, wt


def kernel(x, norm1_g, w_in, lam_q1, lam_k1, lam_q2, lam_k2, subln_g, conv_dw_w, conv_dw_b, conv_ln_g, conv_ln_b, conv_pw_w, conv_pw_b, fnet_w, fnet_b, sgu_ln_g, sgu_ln_b, sgu_w, sgu_b, w_out, norm2_g, w_up, w_down, final_g):
    bsz, seq, d = x.shape
    depth = w_in.shape[0]
    gw = GROUP_WIDTH
    assert seq % TM == 0 and d == D_MODEL

    attn_consts = _attention_consts()
    fnet_consts = _fnet_consts(seq)
    cg = gw // CONV_GROUPS
    gavg = jnp.asarray(np.kron(np.eye(CONV_GROUPS), np.full((cg, cg), 1.0 / cg)), BF16)
    row = lambda v: v.reshape(1, -1).astype(F32)

    for l in range(depth):
        wn, wt = _layer_weights(w_in[l])
        wcat = sgu_w[l].reshape(SGU_GROUPS * SGU_CHUNK, SGU_CHUNK).astype(BF16)
        bias_plane = jnp.repeat(sgu_b[l].T.astype(F32), gw // SGU_GROUPS, axis=1)
        k, hb, hc, yd, qt, vt = _inproj(x, row(norm1_g[l]), wn, wt, row(sgu_ln_g[l]),
                                        row(sgu_ln_b[l]), wcat, bias_plane)

        lam_init = 0.8 - 0.6 * math.exp(-0.3 * l)
        lam_pack = jnp.concatenate(
            [lam_q1[l][None], lam_k1[l][None], lam_q2[l][None], lam_k2[l][None],
             jnp.full((4, ATT_QK), lam_init, F32)], axis=0).astype(F32)
        gcol = (subln_g[l].astype(F32) * (1.0 - lam_init)).reshape(ATT_VDIM, 1)
        ya_t = _attention(lam_pack, gcol, attn_consts, qt, k, vt)

        yb = _conformer_conv(hb, conv_dw_w[l].astype(F32), row(conv_dw_b[l]), row(conv_ln_g[l]),
                             row(conv_ln_b[l]), gavg, conv_pw_w[l].astype(BF16), row(conv_pw_b[l]))

        yc = _fourier_mix(hc, fnet_consts, _block_diag(fnet_w[l]).astype(BF16), row(fnet_b[l]))

        x = _channel_mix(x, ya_t, yb, yc, yd, w_out[l].reshape(4, gw, d).astype(BF16),
                         row(norm2_g[l]), w_up[l].astype(BF16), w_down[l].astype(BF16),
                         row(final_g), l == depth - 1)

    return x
```

```python
import functools
import math

import numpy as np
import jax
import jax.numpy as jnp
from jax import lax
from jax.experimental import pallas as pl
from jax.experimental.pallas import tpu as pltpu

D_MODEL = 1024
GROUP_WIDTH = 256
ATT_HEADS = 4
ATT_VDIM = 64
ATT_QK = 32
CONV_WIDTH = 31
CONV_GROUPS = 4
FNET_GROUPS = 4
SGU_CHUNK = 128
SGU_GROUPS = 4
D_FF = 4 * D_MODEL
EPS = 1e-6
LOG2E = 1.4426950408889634

KPAD = 128
POS_LANE = 2 * ATT_QK
N_SPLIT = 3
TQ = 256
TK = 256
V_ROWS = ATT_VDIM + 16
Q_STREAMS = 2
ITEMS_PER_TRIP = 16
TM = 512
VMEM_LIMIT = 56 * 1024 * 1024

BF16 = jnp.bfloat16
F32 = jnp.float32


def _params(sem, vmem=VMEM_LIMIT):
    return pltpu.CompilerParams(dimension_semantics=sem, vmem_limit_bytes=vmem)


def _full(shape):
    n = len(shape)
    return pl.BlockSpec(shape, lambda *_: (0,) * n)


def _sgu(u, v, g_ref, b_ref, w_ref, bias_ref, o_ref):
    mu = jnp.mean(v, axis=-1, keepdims=True)
    xc = v - mu
    var = jnp.mean(xc * xc, axis=-1, keepdims=True)
    vn = (xc * lax.rsqrt(var + EPS) * g_ref[...] + b_ref[...]).astype(BF16)
    cg = GROUP_WIDTH // SGU_GROUPS
    lane = lax.broadcasted_iota(jnp.int32, (SGU_CHUNK, GROUP_WIDTH), 1)
    w = w_ref[...]
    for c in range(TM // SGU_CHUNK):
        rs = slice(c * SGU_CHUNK, (c + 1) * SGU_CHUNK)
        r = jnp.dot(w, vn[rs], preferred_element_type=F32)
        sv = r[(SGU_GROUPS - 1) * SGU_CHUNK:]
        for g in range(SGU_GROUPS - 2, -1, -1):
            sv = jnp.where(lane < (g + 1) * cg, r[g * SGU_CHUNK:(g + 1) * SGU_CHUNK], sv)
        o_ref[0, rs, :] = (u[rs] * (sv + bias_ref[...])).astype(BF16)


def _inproj_kernel(x_ref, g_ref, wn_ref, wt_ref, sg_ref, sb_ref, sw_ref, sbias_ref,
                   k_ref, hb_ref, hc_ref, yd_ref, qt_ref, vt_ref):
    gw = GROUP_WIDTH
    x = x_ref[0]
    ms = jnp.mean(x * x, axis=-1, keepdims=True)
    xn = (x * lax.rsqrt(ms + EPS) * g_ref[...]).astype(BF16)
    nat = jnp.dot(xn, wn_ref[...], preferred_element_type=F32)
    tr = lax.dot_general(wt_ref[...], xn, (((1,), (1,)), ((), ())),
                         preferred_element_type=F32)
    hw = 2 * ATT_QK
    kz = jnp.zeros((TM, KPAD - hw), F32)
    k_ref[0] = jnp.concatenate(
        [piece for h in range(ATT_HEADS) for piece in (nat[:, h * hw:(h + 1) * hw], kz)],
        axis=1).astype(BF16)
    hb_ref[0] = nat[:, gw:3 * gw]
    hc_ref[0] = nat[:, 3 * gw:4 * gw].astype(BF16)
    _sgu(nat[:, 4 * gw:5 * gw], nat[:, 5 * gw:6 * gw], sg_ref, sb_ref, sw_ref, sbias_ref, yd_ref)
    qscale = (ATT_QK ** -0.5) * LOG2E
    qz = jnp.zeros((KPAD - hw, TM), F32)
    q = jnp.concatenate(
        [piece for h in range(ATT_HEADS) for piece in (tr[h * hw:(h + 1) * hw] * qscale, qz)],
        axis=0).astype(BF16)
    for j in range(TM // TQ):
        qt_ref[0, j] = q[:, j * TQ:(j + 1) * TQ]
    for j in range(TM // TK):
        vt_ref[0, j] = tr[gw:, j * TK:(j + 1) * TK].astype(BF16)


def _inproj(x, g, wn, wt, sgu_g, sgu_b, sgu_w, sgu_bias):
    b, s, d = x.shape
    kw = ATT_HEADS * KPAD
    gw = GROUP_WIDTH
    grid = (b, s // TM)
    tok = lambda w: pl.BlockSpec((1, TM, w), lambda i, j: (i, j, 0))
    out_shape = (
        jax.ShapeDtypeStruct((b, s, kw), BF16),
        jax.ShapeDtypeStruct((b, s, 2 * gw), F32),
        jax.ShapeDtypeStruct((b, s, gw), BF16),
        jax.ShapeDtypeStruct((b, s, gw), BF16),
        jax.ShapeDtypeStruct((b, s // TQ, kw, TQ), BF16),
        jax.ShapeDtypeStruct((b, s // TK, gw, TK), BF16),
    )
    out_specs = (
        tok(kw), tok(2 * gw), tok(gw), tok(gw),
        pl.BlockSpec((1, TM // TQ, kw, TQ), lambda i, j: (i, j, 0, 0)),
        pl.BlockSpec((1, TM // TK, gw, TK), lambda i, j: (i, j, 0, 0)),
    )
    consts = (g, wn, wt, sgu_g, sgu_b, sgu_w, sgu_bias)
    return pl.pallas_call(
        _inproj_kernel,
        out_shape=out_shape,
        grid=grid,
        in_specs=[tok(d)] + [_full(c.shape) for c in consts],
        out_specs=out_specs,
        compiler_params=_params(("parallel", "parallel")),
        name="inproj",
    )(x, *consts)


def _attn_kernel(lam_ref, g_ref, crow_ref, posk_ref, posq_ref, dbias_ref,
                 qt_ref, k_ref, vt_ref, o_ref, kaug_ref, vaug_ref, rhs_ref, sa_ref, sb_ref):
    nq = qt_ref.shape[1]
    nk = vt_ref.shape[1]
    ns = Q_STREAMS

    for j in range(nk):
        kaug_ref[j] = k_ref[0, j * TK:(j + 1) * TK, :] + posk_ref[0]
        vaug_ref[j, :ATT_VDIM, :] = vt_ref[0, j]
        r = lax.broadcasted_iota(jnp.int32, (V_ROWS - ATT_VDIM, TK), 0)
        vaug_ref[j, ATT_VDIM:, :] = jnp.where(r == 0, 1.0, 0.0).astype(BF16)

    lp = lam_ref[...]
    lam_init = lp[4:5, 0:1]
    lam = (jnp.exp(jnp.sum(lp[0:1] * lp[1:2], axis=-1, keepdims=True))
           - jnp.exp(jnp.sum(lp[2:3] * lp[3:4], axis=-1, keepdims=True)) + lam_init)

    crow = crow_ref[0]
    posq = posq_ref[0]
    row = lax.broadcasted_iota(jnp.int32, (KPAD, TQ), 0)

    def key_block(u, qi):
        return jnp.where(u == 0, qi, u - 1 + jnp.where(u - 1 >= qi, 1, 0))

    def scores(buf, g, kj, rhs, diag):
        s = jnp.dot(kaug_ref[kj], rhs, preferred_element_type=F32)
        if diag:
            s = s + dbias_ref[0]
        buf[g] = s
        return jnp.max(s, axis=0, keepdims=True)

    def consume(buf, g, kj, qi, mt, mrun, acc):
        off = crow * (-lax.convert_element_type(jnp.abs(kj - qi) * TK, F32))
        mnew = jnp.maximum(mrun, mt + off)
        alpha = jnp.exp2(mrun - mnew)
        p = jnp.exp2(buf[g] - (mnew - off)).astype(BF16)
        acc = alpha * acc + jnp.dot(vaug_ref[kj], p, preferred_element_type=F32)
        return mnew, acc

    def prologue(qg):
        mts = []
        for g in range(ns):
            qi = qg * ns + g
            qt = qt_ref[0, qi]
            zero = jnp.zeros_like(qt)
            qm = [jnp.where((row >= ATT_QK * m) & (row < ATT_QK * (m + 1)), qt, zero)
                  for m in range(2)]
            rhs_ref[2 * g + 0] = jnp.concatenate([q + posq for q in qm], axis=1)
            rhs_ref[2 * g + 1] = jnp.concatenate([q - posq for q in qm], axis=1)
            mts.append(scores(sa_ref, g, qi, jnp.concatenate(qm, axis=1), True))
        return tuple(mts)

    def produce(buf, qg, u):
        out = []
        for g in range(ns):
            qi = qg * ns + g
            kj = key_block(u, qi)
            side = jnp.where(kj > qi, 1, 0)
            out.append(scores(buf, g, kj, rhs_ref[2 * g + side], False))
        return tuple(out)

    def finish(buf, qg, u, mts, mruns, accs):
        new_mruns, new_accs = [], []
        for g in range(ns):
            qi = qg * ns + g
            mr, ac = consume(buf, g, key_block(u, qi), qi, mts[g], mruns[g], accs[g])
            new_mruns.append(mr)
            new_accs.append(ac)
        return tuple(new_mruns), tuple(new_accs)

    def run_items(qg, first, count, mts, mruns, accs):
        for j in range(count):
            buf, prev = (sb_ref, sa_ref) if j % 2 == 0 else (sa_ref, sb_ref)
            new_mts = produce(buf, qg, first + j)
            mruns, accs = finish(prev, qg, first + j - 1, mts, mruns, accs)
            mts = new_mts
        return mts, mruns, accs

    n_loop = (nk - 1) // ITEMS_PER_TRIP
    n_tail = nk - 1 - n_loop * ITEMS_PER_TRIP

    def qgroup(qg, mts):
        init = (mts,
                tuple(jnp.full((1, 2 * TQ), -1e30, F32) for _ in range(ns)),
                tuple(jnp.zeros((V_ROWS, 2 * TQ), F32) for _ in range(ns)))

        def trip(t, carry):
            return run_items(qg, ITEMS_PER_TRIP * t + 1, ITEMS_PER_TRIP, *carry)

        carry = lax.fori_loop(0, n_loop, trip, init)
        mts, mruns, accs = run_items(qg, n_loop * ITEMS_PER_TRIP + 1, n_tail, *carry)
        last = sb_ref if (nk - 1) % 2 == 1 else sa_ref
        mruns, accs = finish(last, qg, nk - 1, mts, mruns, accs)
        nxt = prologue(jnp.minimum(qg + 1, nq // ns - 1))
        for g in range(ns):
            a = accs[g]
            outs = [a[:ATT_VDIM, m * TQ:(m + 1) * TQ] / a[ATT_VDIM:ATT_VDIM + 1, m * TQ:(m + 1) * TQ]
                    for m in range(2)]
            o = outs[0] - lam * outs[1]
            ms = jnp.mean(o * o, axis=0, keepdims=True)
            o_ref[0, qg * ns + g] = (o * lax.rsqrt(ms + EPS) * g_ref[...]).astype(BF16)
        return nxt

    lax.fori_loop(0, nq // ns, qgroup, prologue(0))


def _attention(lam_pack, gcol, consts, qt, k, vt):
    b, nq = qt.shape[0], qt.shape[1]
    nk = vt.shape[1]
    s = k.shape[1]
    crow, posk, posq, dbias = consts
    head = lambda shape: pl.BlockSpec((1,) + shape, lambda i, h: (h, 0, 0))
    return pl.pallas_call(
        _attn_kernel,
        out_shape=jax.ShapeDtypeStruct((b, nq, GROUP_WIDTH, TQ), BF16),
        grid=(b, ATT_HEADS),
        in_specs=[
            _full(lam_pack.shape), _full(gcol.shape),
            head((1, 2 * TQ)), head((TK, KPAD)), head((KPAD, TQ)), head((TK, 2 * TQ)),
            pl.BlockSpec((1, nq, KPAD, TQ), lambda i, h: (i, 0, h, 0)),
            pl.BlockSpec((1, s, KPAD), lambda i, h: (i, 0, h)),
            pl.BlockSpec((1, nk, ATT_VDIM, TK), lambda i, h: (i, 0, h, 0)),
        ],
        out_specs=pl.BlockSpec((1, nq, ATT_VDIM, TQ), lambda i, h: (i, 0, h, 0)),
        scratch_shapes=[pltpu.VMEM((nk, TK, KPAD), BF16), pltpu.VMEM((nk, V_ROWS, TK), BF16),
                        pltpu.VMEM((2 * Q_STREAMS, KPAD, 2 * TQ), BF16),
                        pltpu.VMEM((Q_STREAMS, TK, 2 * TQ), F32),
                        pltpu.VMEM((Q_STREAMS, TK, 2 * TQ), F32)],
        compiler_params=_params(("parallel", "parallel")),
        name="diff_attention",
    )(lam_pack, gcol, crow, posk, posq, dbias, qt, k, vt)


def _attention_consts():
    slopes = np.array([2.0 ** (-8.0 * (i + 1) / ATT_HEADS) for i in range(ATT_HEADS)], np.float64)
    c = (slopes * LOG2E).astype(np.float32)
    parts = []
    rem = c.astype(np.float32)
    for _ in range(N_SPLIT):
        p = rem.astype(BF16).astype(np.float32)
        parts.append(p)
        rem = (rem - p).astype(np.float32)
    rk = np.arange(TK, dtype=np.float32)
    rq = np.arange(TQ, dtype=np.float32)
    posk = np.zeros((ATT_HEADS, TK, KPAD), np.float32)
    posq = np.zeros((ATT_HEADS, KPAD, TQ), np.float32)
    for t in range(N_SPLIT):
        posk[:, :, POS_LANE + t] = rk[None, :]
        posk[:, :, POS_LANE + N_SPLIT + t] = parts[t][:, None]
        posq[:, POS_LANE + t, :] = parts[t][:, None]
        posq[:, POS_LANE + N_SPLIT + t, :] = -rq[None, :]
    dbias = -c[:, None, None] * np.abs(rq[None, None, :] - rk[None, :, None])
    dbias = np.concatenate([dbias, dbias], axis=2)
    crow = np.broadcast_to(c[:, None, None], (ATT_HEADS, 1, 2 * TQ))
    return (jnp.asarray(crow, F32), jnp.asarray(posk, BF16), jnp.asarray(posq, BF16),
            jnp.asarray(dbias, F32))


CONV_PAD = 16
CONV_ROWS = 256


def _split_dot(x, w):
    hi = x.astype(BF16)
    lo = (x - hi.astype(F32)).astype(BF16)
    return (jnp.dot(hi, w, preferred_element_type=F32)
            + jnp.dot(lo, w, preferred_element_type=F32))


def _conv_kernel(hb_ref, dww_ref, dwb_ref, lng_ref, lnb_ref, gavg_ref, pww_ref, pwb_ref,
                 o_ref, z_ref, sh_ref):
    s = hb_ref.shape[1]
    half = CONV_WIDTH // 2
    zeros = jnp.zeros((CONV_PAD, GROUP_WIDTH), F32)
    z_ref[0:CONV_PAD, :] = zeros
    z_ref[CONV_PAD + s:CONV_PAD + s + CONV_PAD, :] = zeros

    def glu(i, _):
        r0 = pl.multiple_of(i * CONV_ROWS, CONV_ROWS)
        h = hb_ref[0, pl.ds(r0, CONV_ROWS), :]
        z_ref[pl.ds(CONV_PAD + r0, CONV_ROWS), :] = h[:, :GROUP_WIDTH] * jax.nn.sigmoid(h[:, GROUP_WIDTH:])
        return 0

    lax.fori_loop(0, s // CONV_ROWS, glu, 0)

    gavg = gavg_ref[...]

    def chunk(i, _):
        r0 = pl.multiple_of(i * CONV_ROWS, CONV_ROWS)
        acc = jnp.zeros((CONV_ROWS, GROUP_WIDTH), F32) + dwb_ref[...]
        win = z_ref[pl.ds(r0, CONV_ROWS + 2 * CONV_PAD), :]
        span = CONV_ROWS + 2 * CONV_PAD - 8
        for j in range(8):
            sh_ref[j] = win[j:j + span]
            for t in range(CONV_WIDTH):
                off = CONV_PAD - half + t
                if off % 8 == j:
                    a = off - j
                    acc = acc + dww_ref[t:t + 1, :] * sh_ref[j, a:a + CONV_ROWS, :]
        mu = _split_dot(acc, gavg)
        xc = acc - mu
        var = _split_dot(xc * xc, gavg)
        y = xc * lax.rsqrt(var + EPS) * lng_ref[...] + lnb_ref[...]
        y = y * jax.nn.sigmoid(y)
        out = jnp.dot(y.astype(BF16), pww_ref[...], preferred_element_type=F32) + pwb_ref[...]
        o_ref[0, pl.ds(r0, CONV_ROWS), :] = out.astype(BF16)
        return 0

    lax.fori_loop(0, s // CONV_ROWS, chunk, 0, unroll=2)


def _conformer_conv(hb, dww, dwb, lng, lnb, gavg, pww, pwb):
    b, s, _ = hb.shape
    return pl.pallas_call(
        _conv_kernel,
        out_shape=jax.ShapeDtypeStruct((b, s, GROUP_WIDTH), BF16),
        grid=(b,),
        in_specs=[pl.BlockSpec((1, s, 2 * GROUP_WIDTH), lambda i: (i, 0, 0)),
                  _full(dww.shape), _full(dwb.shape), _full(lng.shape), _full(lnb.shape),
                  _full(gavg.shape), _full(pww.shape), _full(pwb.shape)],
        out_specs=pl.BlockSpec((1, s, GROUP_WIDTH), lambda i: (i, 0, 0)),
        scratch_shapes=[pltpu.VMEM((s + 2 * CONV_PAD, GROUP_WIDTH), F32),
                        pltpu.VMEM((8, CONV_ROWS + 2 * CONV_PAD - 8, GROUP_WIDTH), F32)],
        compiler_params=_params(("parallel",)),
        name="conformer_conv",
    )(hb, dww, dwb, lng, lnb, gavg, pww, pwb)


FNET_ROWS = 512
FNET_PAD = 8


def _fnet_kernel(c_ref, cmat_ref, m1_ref, m3_ref, twc_ref, tws_ref, w_ref, b_ref,
                 o_ref, z_ref, bb_ref, y_ref):
    s = c_ref.shape[1]
    n1 = m1_ref.shape[0] // 2
    n2 = m3_ref.shape[0]
    gw = GROUP_WIDTH
    rows = min(FNET_ROWS, s)

    nl = gw // 128
    pz, pb, py = n2 + FNET_PAD, 2 * n2 + FNET_PAD, n1 + FNET_PAD

    for i in range(s // rows):
        z = jnp.dot(c_ref[0, i * rows:(i + 1) * rows, :], cmat_ref[...],
                    preferred_element_type=F32)
        for q in range(rows // n2):
            i1 = i * (rows // n2) + q
            for p in range(2 * nl):
                z_ref[p, i1 * pz:i1 * pz + n2, :] = z[q * n2:(q + 1) * n2, p * 128:(p + 1) * 128]

    m1 = m1_ref[...]
    for j in range(n2):
        zcat = jnp.concatenate(
            [jnp.concatenate([z_ref[comp * nl + p, pl.ds(j, n1, stride=pz), :] for p in range(nl)],
                             axis=1) for comp in range(2)], axis=0)
        a = jnp.dot(m1, zcat.astype(BF16), preferred_element_type=F32)
        tc = twc_ref[j]
        ts = tws_ref[j]
        for p in range(nl):
            ar = a[:n1, p * 128:(p + 1) * 128]
            ai = a[n1:, p * 128:(p + 1) * 128]
            bb_ref[p, pl.ds(j, n1, stride=pb), :] = ar * tc + ai * ts
            bb_ref[p, pl.ds(n2 + j, n1, stride=pb), :] = ai * tc - ar * ts

    m3 = m3_ref[...]
    for k1 in range(n1):
        rhs = jnp.concatenate([bb_ref[p, k1 * pb:k1 * pb + 2 * n2, :] for p in range(nl)],
                              axis=1).astype(BF16)
        y = jnp.dot(m3, rhs, preferred_element_type=F32)
        for p in range(nl):
            y_ref[p, pl.ds(k1, n2, stride=py), :] = y[:, p * 128:(p + 1) * 128]

    norm = 1.0 / math.sqrt(s * (gw // FNET_GROUPS))
    for i in range(s // rows):
        y = jnp.concatenate(
            [jnp.concatenate([y_ref[p, k2 * py:k2 * py + n1, :] for p in range(nl)], axis=1)
             for k2 in range(i * (rows // n1), (i + 1) * (rows // n1))], axis=0)
        out = jnp.dot((y * norm).astype(BF16), w_ref[...], preferred_element_type=F32) + b_ref[...]
        o_ref[0, i * rows:(i + 1) * rows, :] = out.astype(BF16)


def _fnet_consts(s):
    n1 = 1 << (int(math.log2(s)) // 2)
    n2 = s // n1
    cw = GROUP_WIDTH // FNET_GROUPS
    ang = 2.0 * np.pi * np.outer(np.arange(cw), np.arange(cw)) / cw
    eye = np.eye(FNET_GROUPS)
    cmat = np.concatenate([np.kron(eye, np.cos(ang)), -np.kron(eye, np.sin(ang))], axis=1)
    a1 = 2.0 * np.pi * np.outer(np.arange(n1), np.arange(n1)) / n1
    m1 = np.block([[np.cos(a1), np.sin(a1)], [-np.sin(a1), np.cos(a1)]])
    a2 = 2.0 * np.pi * np.outer(np.arange(n2), np.arange(n2)) / n2
    m3 = np.concatenate([np.cos(a2), np.sin(a2)], axis=1)
    at = 2.0 * np.pi * np.outer(np.arange(n2), np.arange(n1)) / s
    twc = np.broadcast_to(np.cos(at)[:, :, None], (n2, n1, 128))
    tws = np.broadcast_to(np.sin(at)[:, :, None], (n2, n1, 128))
    return (jnp.asarray(cmat, BF16), jnp.asarray(m1, BF16), jnp.asarray(m3, BF16),
            jnp.asarray(twc, F32), jnp.asarray(tws, F32))


def _fourier_mix(c, consts, w_bd, bias):
    b, s, gw = c.shape
    cmat, m1, m3, twc, tws = consts
    n1, n2 = m1.shape[0] // 2, m3.shape[0]
    return pl.pallas_call(
        _fnet_kernel,
        out_shape=jax.ShapeDtypeStruct((b, s, gw), BF16),
        grid=(b,),
        in_specs=[pl.BlockSpec((1, s, gw), lambda i: (i, 0, 0)),
                  _full(cmat.shape), _full(m1.shape), _full(m3.shape),
                  _full(twc.shape), _full(tws.shape), _full(w_bd.shape), _full(bias.shape)],
        out_specs=pl.BlockSpec((1, s, gw), lambda i: (i, 0, 0)),
        scratch_shapes=[pltpu.VMEM((2 * gw // 128, n1 * (n2 + FNET_PAD), 128), F32),
                        pltpu.VMEM((gw // 128, n1 * (2 * n2 + FNET_PAD), 128), F32),
                        pltpu.VMEM((gw // 128, n2 * (n1 + FNET_PAD), 128), F32)],
        compiler_params=_params(("parallel",)),
        name="fourier_mix",
    )(c, cmat, m1, m3, twc, tws, w_bd, bias)


FF_CHUNK = 1024


def _channel_kernel(x_ref, ya_ref, yb_ref, yc_ref, yd_ref, wo_ref, g_ref, wu_ref, wd_ref, gf_ref,
                    o_ref, xm_ref, *, final):
    phase = pl.program_id(2)

    @pl.when(phase == 0)
    def _():
        acc = (x_ref[0]
               + jnp.dot(yb_ref[0], wo_ref[1], preferred_element_type=F32)
               + jnp.dot(yc_ref[0], wo_ref[2], preferred_element_type=F32)
               + jnp.dot(yd_ref[0], wo_ref[3], preferred_element_type=F32))
        for j in range(TM // TQ):
            ya = lax.dot_general(ya_ref[0, j], wo_ref[0], (((0,), (0,)), ((), ())),
                                 preferred_element_type=F32)
            xm_ref[j * TQ:(j + 1) * TQ, :] = acc[j * TQ:(j + 1) * TQ] + ya

    @pl.when(phase == 1)
    def _():
        x = xm_ref[...]
        ms = jnp.mean(x * x, axis=-1, keepdims=True)
        xn = (x * lax.rsqrt(ms + EPS) * g_ref[...]).astype(BF16)
        acc = x
        for c in range(D_FF // FF_CHUNK):
            cs = slice(c * FF_CHUNK, (c + 1) * FF_CHUNK)
            h = jnp.maximum(jnp.dot(xn, wu_ref[:, cs], preferred_element_type=F32), 0.0)
            acc = acc + jnp.dot((h * h).astype(BF16), wd_ref[cs, :], preferred_element_type=F32)
        if final:
            ms = jnp.mean(acc * acc, axis=-1, keepdims=True)
            acc = acc * lax.rsqrt(ms + EPS) * gf_ref[...]
        o_ref[0] = acc


def _channel_mix(x, ya_t, yb, yc, yd, wo, g, wu, wd, gf, final):
    b, s, d = x.shape
    gw = GROUP_WIDTH
    tok = lambda width: pl.BlockSpec((1, TM, width), lambda i, j, p: (i, j, 0))
    const = lambda shape: pl.BlockSpec(shape, lambda i, j, p: (0,) * len(shape),
                                       pipeline_mode=pl.Buffered(1))
    return pl.pallas_call(
        functools.partial(_channel_kernel, final=final),
        out_shape=jax.ShapeDtypeStruct((b, s, d), F32),
        grid=(b, s // TM, 2),
        in_specs=[tok(d), pl.BlockSpec((1, TM // TQ, gw, TQ), lambda i, j, p: (i, j, 0, 0)),
                  tok(gw), tok(gw), tok(gw), const(wo.shape), const(g.shape),
                  const(wu.shape), const(wd.shape), const(gf.shape)],
        out_specs=tok(d),
        scratch_shapes=[pltpu.VMEM((TM, d), F32)],
        compiler_params=_params(("parallel", "parallel", "arbitrary")),
        name="channel_mix",
    )(x, ya_t, yb, yc, yd, wo, g, wu, wd, gf)


def _block_diag(blocks):
    g, r, c = blocks.shape
    eye = jnp.eye(g, dtype=blocks.dtype)
    return (eye[:, None, :, None] * blocks[:, :, None, :]).reshape(g * r, g * c)


def _layer_weights(w_in_l):
    gw = GROUP_WIDTH
    wn = w_in_l[:, gw:2 * gw]
    wn = jnp.concatenate([wn, w_in_l[:, 3 * gw:]], axis=1).astype(BF16)
    wt = jnp.concatenate([w_in_l[:, :gw], w_in_l[:, 2 * gw:3 * gw]], axis=1).T.astype(BF16)
    return wn, wt


def kernel(x, norm1_g, w_in, lam_q1, lam_k1, lam_q2, lam_k2, subln_g, conv_dw_w, conv_dw_b, conv_ln_g, conv_ln_b, conv_pw_w, conv_pw_b, fnet_w, fnet_b, sgu_ln_g, sgu_ln_b, sgu_w, sgu_b, w_out, norm2_g, w_up, w_down, final_g):
    bsz, seq, d = x.shape
    depth = w_in.shape[0]
    gw = GROUP_WIDTH
    assert seq % TM == 0 and d == D_MODEL

    attn_consts = _attention_consts()
    fnet_consts = _fnet_consts(seq)
    cg = gw // CONV_GROUPS
    gavg = jnp.asarray(np.kron(np.eye(CONV_GROUPS), np.full((cg, cg), 1.0 / cg)), BF16)
    row = lambda v: v.reshape(1, -1).astype(F32)

    for l in range(depth):
        wn, wt = _layer_weights(w_in[l])
        wcat = sgu_w[l].reshape(SGU_GROUPS * SGU_CHUNK, SGU_CHUNK).astype(BF16)
        bias_plane = jnp.repeat(sgu_b[l].T.astype(F32), gw // SGU_GROUPS, axis=1)
        k, hb, hc, yd, qt, vt = _inproj(x, row(norm1_g[l]), wn, wt, row(sgu_ln_g[l]),
                                        row(sgu_ln_b[l]), wcat, bias_plane)

        lam_init = 0.8 - 0.6 * math.exp(-0.3 * l)
        lam_pack = jnp.concatenate(
            [lam_q1[l][None], lam_k1[l][None], lam_q2[l][None], lam_k2[l][None],
             jnp.full((4, ATT_QK), lam_init, F32)], axis=0).astype(F32)
        gcol = (subln_g[l].astype(F32) * (1.0 - lam_init)).reshape(ATT_VDIM, 1)
        ya_t = _attention(lam_pack, gcol, attn_consts, qt, k, vt)

        yb = _conformer_conv(hb, conv_dw_w[l].astype(F32), row(conv_dw_b[l]), row(conv_ln_g[l]),
                             row(conv_ln_b[l]), gavg, conv_pw_w[l].astype(BF16), row(conv_pw_b[l]))

        yc = _fourier_mix(hc, fnet_consts, _block_diag(fnet_w[l]).astype(BF16), row(fnet_b[l]))

        x = _channel_mix(x, ya_t, yb, yc, yd, w_out[l].reshape(4, gw, d).astype(BF16),
                         row(norm2_g[l]), w_up[l].astype(BF16), w_down[l].astype(BF16),
                         row(final_g), l == depth - 1)

    return x
```

```python
import functools
import math

import numpy as np
import jax
import jax.numpy as jnp
from jax import lax
from jax.experimental import pallas as pl
from jax.experimental.pallas import tpu as pltpu

D_MODEL = 1024
GROUP_WIDTH = 256
ATT_HEADS = 4
ATT_VDIM = 64
ATT_QK = 32
CONV_WIDTH = 31
CONV_GROUPS = 4
FNET_GROUPS = 4
SGU_CHUNK = 128
SGU_GROUPS = 4
D_FF = 4 * D_MODEL
EPS = 1e-6
LOG2E = 1.4426950408889634

KPAD = 128
POS_LANE = 2 * ATT_QK
N_SPLIT = 3
TQ = 256
TK = 256
V_ROWS = ATT_VDIM + 16
Q_STREAMS = 2
ITEMS_PER_TRIP = 16
TM = 512
VMEM_LIMIT = 56 * 1024 * 1024

BF16 = jnp.bfloat16
F32 = jnp.float32


def _params(sem, vmem=VMEM_LIMIT):
    return pltpu.CompilerParams(dimension_semantics=sem, vmem_limit_bytes=vmem)


def _full(shape):
    n = len(shape)
    return pl.BlockSpec(shape, lambda *_: (0,) * n)


def _sgu(u, v, g_ref, b_ref, w_ref, bias_ref, o_ref):
    mu = jnp.mean(v, axis=-1, keepdims=True)
    xc = v - mu
    var = jnp.mean(xc * xc, axis=-1, keepdims=True)
    vn = (xc * lax.rsqrt(var + EPS) * g_ref[...] + b_ref[...]).astype(BF16)
    cg = GROUP_WIDTH // SGU_GROUPS
    lane = lax.broadcasted_iota(jnp.int32, (SGU_CHUNK, GROUP_WIDTH), 1)
    w = w_ref[...]
    for c in range(TM // SGU_CHUNK):
        rs = slice(c * SGU_CHUNK, (c + 1) * SGU_CHUNK)
        r = jnp.dot(w, vn[rs], preferred_element_type=F32)
        sv = r[(SGU_GROUPS - 1) * SGU_CHUNK:]
        for g in range(SGU_GROUPS - 2, -1, -1):
            sv = jnp.where(lane < (g + 1) * cg, r[g * SGU_CHUNK:(g + 1) * SGU_CHUNK], sv)
        o_ref[0, rs, :] = (u[rs] * (sv + bias_ref[...])).astype(BF16)


def _inproj_kernel(x_ref, g_ref, wn_ref, wt_ref, sg_ref, sb_ref, sw_ref, sbias_ref,
                   k_ref, hb_ref, hc_ref, yd_ref, qt_ref, vt_ref):
    gw = GROUP_WIDTH
    x = x_ref[0]
    ms = jnp.mean(x * x, axis=-1, keepdims=True)
    xn = (x * lax.rsqrt(ms + EPS) * g_ref[...]).astype(BF16)
    nat = jnp.dot(xn, wn_ref[...], preferred_element_type=F32)
    tr = lax.dot_general(wt_ref[...], xn, (((1,), (1,)), ((), ())),
                         preferred_element_type=F32)
    hw = 2 * ATT_QK
    kz = jnp.zeros((TM, KPAD - hw), F32)
    k_ref[0] = jnp.concatenate(
        [piece for h in range(ATT_HEADS) for piece in (nat[:, h * hw:(h + 1) * hw], kz)],
        axis=1).astype(BF16)
    hb_ref[0] = nat[:, gw:3 * gw]
    hc_ref[0] = nat[:, 3 * gw:4 * gw].astype(BF16)
    _sgu(nat[:, 4 * gw:5 * gw], nat[:, 5 * gw:6 * gw], sg_ref, sb_ref, sw_ref, sbias_ref, yd_ref)
    qscale = (ATT_QK ** -0.5) * LOG2E
    qz = jnp.zeros((KPAD - hw, TM), F32)
    q = jnp.concatenate(
        [piece for h in range(ATT_HEADS) for piece in (tr[h * hw:(h + 1) * hw] * qscale, qz)],
        axis=0).astype(BF16)
    for j in range(TM // TQ):
        qt_ref[0, j] = q[:, j * TQ:(j + 1) * TQ]
    for j in range(TM // TK):
        vt_ref[0, j] = tr[gw:, j * TK:(j + 1) * TK].astype(BF16)


def _inproj(x, g, wn, wt, sgu_g, sgu_b, sgu_w, sgu_bias):
    b, s, d = x.shape
    kw = ATT_HEADS * KPAD
    gw = GROUP_WIDTH
    grid = (b, s // TM)
    tok = lambda w: pl.BlockSpec((1, TM, w), lambda i, j: (i, j, 0))
    out_shape = (
        jax.ShapeDtypeStruct((b, s, kw), BF16),
        jax.ShapeDtypeStruct((b, s, 2 * gw), F32),
        jax.ShapeDtypeStruct((b, s, gw), BF16),
        jax.ShapeDtypeStruct((b, s, gw), BF16),
        jax.ShapeDtypeStruct((b, s // TQ, kw, TQ), BF16),
        jax.ShapeDtypeStruct((b, s // TK, gw, TK), BF16),
    )
    out_specs = (
        tok(kw), tok(2 * gw), tok(gw), tok(gw),
        pl.BlockSpec((1, TM // TQ, kw, TQ), lambda i, j: (i, j, 0, 0)),
        pl.BlockSpec((1, TM // TK, gw, TK), lambda i, j: (i, j, 0, 0)),
    )
    consts = (g, wn, wt, sgu_g, sgu_b, sgu_w, sgu_bias)
    return pl.pallas_call(
        _inproj_kernel,
        out_shape=out_shape,
        grid=grid,
        in_specs=[tok(d)] + [_full(c.shape) for c in consts],
        out_specs=out_specs,
        compiler_params=_params(("parallel", "parallel")),
        name="inproj",
    )(x, *consts)


def _attn_kernel(lam_ref, g_ref, crow_ref, posk_ref, posq_ref, dbias_ref,
                 qt_ref, k_ref, vt_ref, o_ref, kaug_ref, vaug_ref, rhs_ref, sa_ref, sb_ref):
    nq = qt_ref.shape[1]
    nk = vt_ref.shape[1]
    ns = Q_STREAMS

    for j in range(nk):
        kaug_ref[j] = k_ref[0, j * TK:(j + 1) * TK, :] + posk_ref[0]
        vaug_ref[j, :ATT_VDIM, :] = vt_ref[0, j]
        r = lax.broadcasted_iota(jnp.int32, (V_ROWS - ATT_VDIM, TK), 0)
        vaug_ref[j, ATT_VDIM:, :] = jnp.where(r == 0, 1.0, 0.0).astype(BF16)

    lp = lam_ref[...]
    lam_init = lp[4:5, 0:1]
    lam = (jnp.exp(jnp.sum(lp[0:1] * lp[1:2], axis=-1, keepdims=True))
           - jnp.exp(jnp.sum(lp[2:3] * lp[3:4], axis=-1, keepdims=True)) + lam_init)

    crow = crow_ref[0]
    posq = posq_ref[0]
    row = lax.broadcasted_iota(jnp.int32, (KPAD, TQ), 0)

    def key_block(u, qi):
        return jnp.where(u == 0, qi, u - 1 + jnp.where(u - 1 >= qi, 1, 0))

    def scores(buf, g, kj, rhs, diag):
        s = jnp.dot(kaug_ref[kj], rhs, preferred_element_type=F32)
        if diag:
            s = s + dbias_ref[0]
        buf[g] = s
        return jnp.max(s, axis=0, keepdims=True)

    def consume(buf, g, kj, qi, mt, mrun, acc):
        off = crow * (-lax.convert_element_type(jnp.abs(kj - qi) * TK, F32))
        mnew = jnp.maximum(mrun, mt + off)
        alpha = jnp.exp2(mrun - mnew)
        p = jnp.exp2(buf[g] - (mnew - off)).astype(BF16)
        acc = alpha * acc + jnp.dot(vaug_ref[kj], p, preferred_element_type=F32)
        return mnew, acc

    def prologue(qg):
        mts = []
        for g in range(ns):
            qi = qg * ns + g
            qt = qt_ref[0, qi]
            zero = jnp.zeros_like(qt)
            qm = [jnp.where((row >= ATT_QK * m) & (row < ATT_QK * (m + 1)), qt, zero)
                  for m in range(2)]
            rhs_ref[2 * g + 0] = jnp.concatenate([q + posq for q in qm], axis=1)
            rhs_ref[2 * g + 1] = jnp.concatenate([q - posq for q in qm], axis=1)
            mts.append(scores(sa_ref, g, qi, jnp.concatenate(qm, axis=1), True))
        return tuple(mts)

    def produce(buf, qg, u):
        out = []
        for g in range(ns):
            qi = qg * ns + g
            kj = key_block(u, qi)
            side = jnp.where(kj > qi, 1, 0)
            out.append(scores(buf, g, kj, rhs_ref[2 * g + side], False))
        return tuple(out)

    def finish(buf, qg, u, mts, mruns, accs):
        new_mruns, new_accs = [], []
        for g in range(ns):
            qi = qg * ns + g
            mr, ac = consume(buf, g, key_block(u, qi), qi, mts[g], mruns[g], accs[g])
            new_mruns.append(mr)
            new_accs.append(ac)
        return tuple(new_mruns), tuple(new_accs)

    def run_items(qg, first, count, mts, mruns, accs):
        for j in range(count):
            buf, prev = (sb_ref, sa_ref) if j % 2 == 0 else (sa_ref, sb_ref)
            new_mts = produce(buf, qg, first + j)
            mruns, accs = finish(prev, qg, first + j - 1, mts, mruns, accs)
            mts = new_mts
        return mts, mruns, accs

    n_loop = (nk - 1) // ITEMS_PER_TRIP
    n_tail = nk - 1 - n_loop * ITEMS_PER_TRIP

    def qgroup(qg, mts):
        init = (mts,
                tuple(jnp.full((1, 2 * TQ), -1e30, F32) for _ in range(ns)),
                tuple(jnp.zeros((V_ROWS, 2 * TQ), F32) for _ in range(ns)))

        def trip(t, carry):
            return run_items(qg, ITEMS_PER_TRIP * t + 1, ITEMS_PER_TRIP, *carry)

        carry = lax.fori_loop(0, n_loop, trip, init)
        mts, mruns, accs = run_items(qg, n_loop * ITEMS_PER_TRIP + 1, n_tail, *carry)
        last = sb_ref if (nk - 1) % 2 == 1 else sa_ref
        mruns, accs = finish(last, qg, nk - 1, mts, mruns, accs)
        nxt = prologue(jnp.minimum(qg + 1, nq // ns - 1))
        for g in range(ns):
            a = accs[g]
            outs = [a[:ATT_VDIM, m * TQ:(m + 1) * TQ] / a[ATT_VDIM:ATT_VDIM + 1, m * TQ:(m + 1) * TQ]
                    for m in range(2)]
            o = outs[0] - lam * outs[1]
            ms = jnp.mean(o * o, axis=0, keepdims=True)
            o_ref[0, qg * ns + g] = (o * lax.rsqrt(ms + EPS) * g_ref[...]).astype(BF16)
        return nxt

    lax.fori_loop(0, nq // ns, qgroup, prologue(0), unroll=2)


def _attention(lam_pack, gcol, consts, qt, k, vt):
    b, nq = qt.shape[0], qt.shape[1]
    nk = vt.shape[1]
    s = k.shape[1]
    crow, posk, posq, dbias = consts
    head = lambda shape: pl.BlockSpec((1,) + shape, lambda i, h: (h, 0, 0))
    return pl.pallas_call(
        _attn_kernel,
        out_shape=jax.ShapeDtypeStruct((b, nq, GROUP_WIDTH, TQ), BF16),
        grid=(b, ATT_HEADS),
        in_specs=[
            _full(lam_pack.shape), _full(gcol.shape),
            head((1, 2 * TQ)), head((TK, KPAD)), head((KPAD, TQ)), head((TK, 2 * TQ)),
            pl.BlockSpec((1, nq, KPAD, TQ), lambda i, h: (i, 0, h, 0)),
            pl.BlockSpec((1, s, KPAD), lambda i, h: (i, 0, h)),
            pl.BlockSpec((1, nk, ATT_VDIM, TK), lambda i, h: (i, 0, h, 0)),
        ],
        out_specs=pl.BlockSpec((1, nq, ATT_VDIM, TQ), lambda i, h: (i, 0, h, 0)),
        scratch_shapes=[pltpu.VMEM((nk, TK, KPAD), BF16), pltpu.VMEM((nk, V_ROWS, TK), BF16),
                        pltpu.VMEM((2 * Q_STREAMS, KPAD, 2 * TQ), BF16),
                        pltpu.VMEM((Q_STREAMS, TK, 2 * TQ), F32),
                        pltpu.VMEM((Q_STREAMS, TK, 2 * TQ), F32)],
        compiler_params=_params(("parallel", "parallel")),
        name="diff_attention",
    )(lam_pack, gcol, crow, posk, posq, dbias, qt, k, vt)


def _attention_consts():
    slopes = np.array([2.0 ** (-8.0 * (i + 1) / ATT_HEADS) for i in range(ATT_HEADS)], np.float64)
    c = (slopes * LOG2E).astype(np.float32)
    parts = []
    rem = c.astype(np.float32)
    for _ in range(N_SPLIT):
        p = rem.astype(BF16).astype(np.float32)
        parts.append(p)
        rem = (rem - p).astype(np.float32)
    rk = np.arange(TK, dtype=np.float32)
    rq = np.arange(TQ, dtype=np.float32)
    posk = np.zeros((ATT_HEADS, TK, KPAD), np.float32)
    posq = np.zeros((ATT_HEADS, KPAD, TQ), np.float32)
    for t in range(N_SPLIT):
        posk[:, :, POS_LANE + t] = rk[None, :]
        posk[:, :, POS_LANE + N_SPLIT + t] = parts[t][:, None]
        posq[:, POS_LANE + t, :] = parts[t][:, None]
        posq[:, POS_LANE + N_SPLIT + t, :] = -rq[None, :]
    dbias = -c[:, None, None] * np.abs(rq[None, None, :] - rk[None, :, None])
    dbias = np.concatenate([dbias, dbias], axis=2)
    crow = np.broadcast_to(c[:, None, None], (ATT_HEADS, 1, 2 * TQ))
    return (jnp.asarray(crow, F32), jnp.asarray(posk, BF16), jnp.asarray(posq, BF16),
            jnp.asarray(dbias, F32))


CONV_PAD = 16
CONV_ROWS = 256


def _split_dot(x, w):
    hi = x.astype(BF16)
    lo = (x - hi.astype(F32)).astype(BF16)
    return (jnp.dot(hi, w, preferred_element_type=F32)
            + jnp.dot(lo, w, preferred_element_type=F32))


def _conv_kernel(hb_ref, dww_ref, dwb_ref, lng_ref, lnb_ref, gavg_ref, pww_ref, pwb_ref,
                 o_ref, z_ref, sh_ref):
    s = hb_ref.shape[1]
    half = CONV_WIDTH // 2
    zeros = jnp.zeros((CONV_PAD, GROUP_WIDTH), F32)
    z_ref[0:CONV_PAD, :] = zeros
    z_ref[CONV_PAD + s:CONV_PAD + s + CONV_PAD, :] = zeros

    def glu(i, _):
        r0 = pl.multiple_of(i * CONV_ROWS, CONV_ROWS)
        h = hb_ref[0, pl.ds(r0, CONV_ROWS), :]
        z_ref[pl.ds(CONV_PAD + r0, CONV_ROWS), :] = h[:, :GROUP_WIDTH] * jax.nn.sigmoid(h[:, GROUP_WIDTH:])
        return 0

    lax.fori_loop(0, s // CONV_ROWS, glu, 0)

    gavg = gavg_ref[...]

    def chunk(i, _):
        r0 = pl.multiple_of(i * CONV_ROWS, CONV_ROWS)
        acc = jnp.zeros((CONV_ROWS, GROUP_WIDTH), F32) + dwb_ref[...]
        win = z_ref[pl.ds(r0, CONV_ROWS + 2 * CONV_PAD), :]
        span = CONV_ROWS + 2 * CONV_PAD - 8
        for j in range(8):
            sh_ref[j] = win[j:j + span]
            for t in range(CONV_WIDTH):
                off = CONV_PAD - half + t
                if off % 8 == j:
                    a = off - j
                    acc = acc + dww_ref[t:t + 1, :] * sh_ref[j, a:a + CONV_ROWS, :]
        mu = _split_dot(acc, gavg)
        xc = acc - mu
        var = _split_dot(xc * xc, gavg)
        y = xc * lax.rsqrt(var + EPS) * lng_ref[...] + lnb_ref[...]
        y = y * jax.nn.sigmoid(y)
        out = jnp.dot(y.astype(BF16), pww_ref[...], preferred_element_type=F32) + pwb_ref[...]
        o_ref[0, pl.ds(r0, CONV_ROWS), :] = out.astype(BF16)
        return 0

    lax.fori_loop(0, s // CONV_ROWS, chunk, 0, unroll=2)


def _conformer_conv(hb, dww, dwb, lng, lnb, gavg, pww, pwb):
    b, s, _ = hb.shape
    return pl.pallas_call(
        _conv_kernel,
        out_shape=jax.ShapeDtypeStruct((b, s, GROUP_WIDTH), BF16),
        grid=(b,),
        in_specs=[pl.BlockSpec((1, s, 2 * GROUP_WIDTH), lambda i: (i, 0, 0)),
                  _full(dww.shape), _full(dwb.shape), _full(lng.shape), _full(lnb.shape),
                  _full(gavg.shape), _full(pww.shape), _full(pwb.shape)],
        out_specs=pl.BlockSpec((1, s, GROUP_WIDTH), lambda i: (i, 0, 0)),
        scratch_shapes=[pltpu.VMEM((s + 2 * CONV_PAD, GROUP_WIDTH), F32),
                        pltpu.VMEM((8, CONV_ROWS + 2 * CONV_PAD - 8, GROUP_WIDTH), F32)],
        compiler_params=_params(("parallel",)),
        name="conformer_conv",
    )(hb, dww, dwb, lng, lnb, gavg, pww, pwb)


FNET_ROWS = 512
FNET_PAD = 8


def _fnet_kernel(c_ref, cmat_ref, m1_ref, m3_ref, twc_ref, tws_ref, w_ref, b_ref,
                 o_ref, z_ref, bb_ref, y_ref):
    s = c_ref.shape[1]
    n1 = m1_ref.shape[0] // 2
    n2 = m3_ref.shape[0]
    gw = GROUP_WIDTH
    rows = min(FNET_ROWS, s)

    nl = gw // 128
    pz, pb, py = n2 + FNET_PAD, 2 * n2 + FNET_PAD, n1 + FNET_PAD

    for i in range(s // rows):
        z = jnp.dot(c_ref[0, i * rows:(i + 1) * rows, :], cmat_ref[...],
                    preferred_element_type=F32)
        for q in range(rows // n2):
            i1 = i * (rows // n2) + q
            for p in range(2 * nl):
                z_ref[p, i1 * pz:i1 * pz + n2, :] = z[q * n2:(q + 1) * n2, p * 128:(p + 1) * 128]

    m1 = m1_ref[...]
    for j in range(n2):
        zcat = jnp.concatenate(
            [jnp.concatenate([z_ref[comp * nl + p, pl.ds(j, n1, stride=pz), :] for p in range(nl)],
                             axis=1) for comp in range(2)], axis=0)
        a = jnp.dot(m1, zcat.astype(BF16), preferred_element_type=F32)
        tc = twc_ref[j]
        ts = tws_ref[j]
        for p in range(nl):
            ar = a[:n1, p * 128:(p + 1) * 128]
            ai = a[n1:, p * 128:(p + 1) * 128]
            bb_ref[p, pl.ds(j, n1, stride=pb), :] = ar * tc + ai * ts
            bb_ref[p, pl.ds(n2 + j, n1, stride=pb), :] = ai * tc - ar * ts

    m3 = m3_ref[...]
    for k1 in range(n1):
        rhs = jnp.concatenate([bb_ref[p, k1 * pb:k1 * pb + 2 * n2, :] for p in range(nl)],
                              axis=1).astype(BF16)
        y = jnp.dot(m3, rhs, preferred_element_type=F32)
        for p in range(nl):
            y_ref[p, pl.ds(k1, n2, stride=py), :] = y[:, p * 128:(p + 1) * 128]

    norm = 1.0 / math.sqrt(s * (gw // FNET_GROUPS))
    for i in range(s // rows):
        y = jnp.concatenate(
            [jnp.concatenate([y_ref[p, k2 * py:k2 * py + n1, :] for p in range(nl)], axis=1)
             for k2 in range(i * (rows // n1), (i + 1) * (rows // n1))], axis=0)
        out = jnp.dot((y * norm).astype(BF16), w_ref[...], preferred_element_type=F32) + b_ref[...]
        o_ref[0, i * rows:(i + 1) * rows, :] = out.astype(BF16)


def _fnet_consts(s):
    n1 = 1 << (int(math.log2(s)) // 2)
    n2 = s // n1
    cw = GROUP_WIDTH // FNET_GROUPS
    ang = 2.0 * np.pi * np.outer(np.arange(cw), np.arange(cw)) / cw
    eye = np.eye(FNET_GROUPS)
    cmat = np.concatenate([np.kron(eye, np.cos(ang)), -np.kron(eye, np.sin(ang))], axis=1)
    a1 = 2.0 * np.pi * np.outer(np.arange(n1), np.arange(n1)) / n1
    m1 = np.block([[np.cos(a1), np.sin(a1)], [-np.sin(a1), np.cos(a1)]])
    a2 = 2.0 * np.pi * np.outer(np.arange(n2), np.arange(n2)) / n2
    m3 = np.concatenate([np.cos(a2), np.sin(a2)], axis=1)
    at = 2.0 * np.pi * np.outer(np.arange(n2), np.arange(n1)) / s
    twc = np.broadcast_to(np.cos(at)[:, :, None], (n2, n1, 128))
    tws = np.broadcast_to(np.sin(at)[:, :, None], (n2, n1, 128))
    return (jnp.asarray(cmat, BF16), jnp.asarray(m1, BF16), jnp.asarray(m3, BF16),
            jnp.asarray(twc, F32), jnp.asarray(tws, F32))


def _fourier_mix(c, consts, w_bd, bias):
    b, s, gw = c.shape
    cmat, m1, m3, twc, tws = consts
    n1, n2 = m1.shape[0] // 2, m3.shape[0]
    return pl.pallas_call(
        _fnet_kernel,
        out_shape=jax.ShapeDtypeStruct((b, s, gw), BF16),
        grid=(b,),
        in_specs=[pl.BlockSpec((1, s, gw), lambda i: (i, 0, 0)),
                  _full(cmat.shape), _full(m1.shape), _full(m3.shape),
                  _full(twc.shape), _full(tws.shape), _full(w_bd.shape), _full(bias.shape)],
        out_specs=pl.BlockSpec((1, s, gw), lambda i: (i, 0, 0)),
        scratch_shapes=[pltpu.VMEM((2 * gw // 128, n1 * (n2 + FNET_PAD), 128), F32),
                        pltpu.VMEM((gw // 128, n1 * (2 * n2 + FNET_PAD), 128), F32),
                        pltpu.VMEM((gw // 128, n2 * (n1 + FNET_PAD), 128), F32)],
        compiler_params=_params(("parallel",)),
        name="fourier_mix",
    )(c, cmat, m1, m3, twc, tws, w_bd, bias)


FF_CHUNK = 1024


def _channel_kernel(x_ref, ya_ref, yb_ref, yc_ref, yd_ref, wo_ref, g_ref, wu_ref, wd_ref, gf_ref,
                    o_ref, xm_ref, *, final):
    phase = pl.program_id(2)

    @pl.when(phase == 0)
    def _():
        acc = (x_ref[0]
               + jnp.dot(yb_ref[0], wo_ref[1], preferred_element_type=F32)
               + jnp.dot(yc_ref[0], wo_ref[2], preferred_element_type=F32)
               + jnp.dot(yd_ref[0], wo_ref[3], preferred_element_type=F32))
        for j in range(TM // TQ):
            ya = lax.dot_general(ya_ref[0, j], wo_ref[0], (((0,), (0,)), ((), ())),
                                 preferred_element_type=F32)
            xm_ref[j * TQ:(j + 1) * TQ, :] = acc[j * TQ:(j + 1) * TQ] + ya

    @pl.when(phase == 1)
    def _():
        x = xm_ref[...]
        ms = jnp.mean(x * x, axis=-1, keepdims=True)
        xn = (x * lax.rsqrt(ms + EPS) * g_ref[...]).astype(BF16)
        acc = x
        for c in range(D_FF // FF_CHUNK):
            cs = slice(c * FF_CHUNK, (c + 1) * FF_CHUNK)
            h = jnp.maximum(jnp.dot(xn, wu_ref[:, cs], preferred_element_type=F32), 0.0)
            acc = acc + jnp.dot((h * h).astype(BF16), wd_ref[cs, :], preferred_element_type=F32)
        if final:
            ms = jnp.mean(acc * acc, axis=-1, keepdims=True)
            acc = acc * lax.rsqrt(ms + EPS) * gf_ref[...]
        o_ref[0] = acc


def _channel_mix(x, ya_t, yb, yc, yd, wo, g, wu, wd, gf, final):
    b, s, d = x.shape
    gw = GROUP_WIDTH
    tok = lambda width: pl.BlockSpec((1, TM, width), lambda i, j, p: (i, j, 0))
    const = lambda shape: pl.BlockSpec(shape, lambda i, j, p: (0,) * len(shape),
                                       pipeline_mode=pl.Buffered(1))
    return pl.pallas_call(
        functools.partial(_channel_kernel, final=final),
        out_shape=jax.ShapeDtypeStruct((b, s, d), F32),
        grid=(b, s // TM, 2),
        in_specs=[tok(d), pl.BlockSpec((1, TM // TQ, gw, TQ), lambda i, j, p: (i, j, 0, 0)),
                  tok(gw), tok(gw), tok(gw), const(wo.shape), const(g.shape),
                  const(wu.shape), const(wd.shape), const(gf.shape)],
        out_specs=tok(d),
        scratch_shapes=[pltpu.VMEM((TM, d), F32)],
        compiler_params=_params(("parallel", "parallel", "arbitrary")),
        name="channel_mix",
    )(x, ya_t, yb, yc, yd, wo, g, wu, wd, gf)


def _block_diag(blocks):
    g, r, c = blocks.shape
    eye = jnp.eye(g, dtype=blocks.dtype)
    return (eye[:, None, :, None] * blocks[:, :, None, :]).reshape(g * r, g * c)


def _layer_weights(w_in_l):
    gw = GROUP_WIDTH
    wn = w_in_l[:, gw:2 * gw]
    wn = jnp.concatenate([wn, w_in_l[:, 3 * gw:]], axis=1).astype(BF16)
    wt = jnp.concatenate([w_in_l[:, :gw], w_in_l[:, 2 * gw:3 * gw]], axis=1).T.astype(BF16)
    return wn, wt


def kernel(x, norm1_g, w_in, lam_q1, lam_k1, lam_q2, lam_k2, subln_g, conv_dw_w, conv_dw_b, conv_ln_g, conv_ln_b, conv_pw_w, conv_pw_b, fnet_w, fnet_b, sgu_ln_g, sgu_ln_b, sgu_w, sgu_b, w_out, norm2_g, w_up, w_down, final_g):
    bsz, seq, d = x.shape
    depth = w_in.shape[0]
    gw = GROUP_WIDTH
    assert seq % TM == 0 and d == D_MODEL

    attn_consts = _attention_consts()
    fnet_consts = _fnet_consts(seq)
    cg = gw // CONV_GROUPS
    gavg = jnp.asarray(np.kron(np.eye(CONV_GROUPS), np.full((cg, cg), 1.0 / cg)), BF16)
    row = lambda v: v.reshape(1, -1).astype(F32)

    for l in range(depth):
        wn, wt = _layer_weights(w_in[l])
        wcat = sgu_w[l].reshape(SGU_GROUPS * SGU_CHUNK, SGU_CHUNK).astype(BF16)
        bias_plane = jnp.repeat(sgu_b[l].T.astype(F32), gw // SGU_GROUPS, axis=1)
        k, hb, hc, yd, qt, vt = _inproj(x, row(norm1_g[l]), wn, wt, row(sgu_ln_g[l]),
                                        row(sgu_ln_b[l]), wcat, bias_plane)

        lam_init = 0.8 - 0.6 * math.exp(-0.3 * l)
        lam_pack = jnp.concatenate(
            [lam_q1[l][None], lam_k1[l][None], lam_q2[l][None], lam_k2[l][None],
             jnp.full((4, ATT_QK), lam_init, F32)], axis=0).astype(F32)
        gcol = (subln_g[l].astype(F32) * (1.0 - lam_init)).reshape(ATT_VDIM, 1)
        ya_t = _attention(lam_pack, gcol, attn_consts, qt, k, vt)

        yb = _conformer_conv(hb, conv_dw_w[l].astype(F32), row(conv_dw_b[l]), row(conv_ln_g[l]),
                             row(conv_ln_b[l]), gavg, conv_pw_w[l].astype(BF16), row(conv_pw_b[l]))

        yc = _fourier_mix(hc, fnet_consts, _block_diag(fnet_w[l]).astype(BF16), row(fnet_b[l]))

        x = _channel_mix(x, ya_t, yb, yc, yd, w_out[l].reshape(4, gw, d).astype(BF16),
                         row(norm2_g[l]), w_up[l].astype(BF16), w_down[l].astype(BF16),
                         row(final_g), l == depth - 1)

    return x
```

```python
import functools
import math

import numpy as np
import jax
import jax.numpy as jnp
from jax import lax
from jax.experimental import pallas as pl
from jax.experimental.pallas import tpu as pltpu

D_MODEL = 1024
GROUP_WIDTH = 256
ATT_HEADS = 4
ATT_VDIM = 64
ATT_QK = 32
CONV_WIDTH = 31
CONV_GROUPS = 4
FNET_GROUPS = 4
SGU_CHUNK = 128
SGU_GROUPS = 4
D_FF = 4 * D_MODEL
EPS = 1e-6
LOG2E = 1.4426950408889634

KPAD = 128
POS_LANE = 2 * ATT_QK
N_SPLIT = 3
TQ = 256
TK = 256
V_ROWS = ATT_VDIM + 16
Q_STREAMS = 2
ITEMS_PER_TRIP = 16
TM = 512
VMEM_LIMIT = 56 * 1024 * 1024

BF16 = jnp.bfloat16
F32 = jnp.float32


def _params(sem, vmem=VMEM_LIMIT):
    return pltpu.CompilerParams(dimension_semantics=sem, vmem_limit_bytes=vmem)


def _full(shape):
    n = len(shape)
    return pl.BlockSpec(shape, lambda *_: (0,) * n)


def _sgu(u, v, g_ref, b_ref, w_ref, bias_ref, o_ref):
    mu = jnp.mean(v, axis=-1, keepdims=True)
    xc = v - mu
    var = jnp.mean(xc * xc, axis=-1, keepdims=True)
    vn = (xc * lax.rsqrt(var + EPS) * g_ref[...] + b_ref[...]).astype(BF16)
    cg = GROUP_WIDTH // SGU_GROUPS
    lane = lax.broadcasted_iota(jnp.int32, (SGU_CHUNK, GROUP_WIDTH), 1)
    w = w_ref[...]
    for c in range(TM // SGU_CHUNK):
        rs = slice(c * SGU_CHUNK, (c + 1) * SGU_CHUNK)
        r = jnp.dot(w, vn[rs], preferred_element_type=F32)
        sv = r[(SGU_GROUPS - 1) * SGU_CHUNK:]
        for g in range(SGU_GROUPS - 2, -1, -1):
            sv = jnp.where(lane < (g + 1) * cg, r[g * SGU_CHUNK:(g + 1) * SGU_CHUNK], sv)
        o_ref[0, rs, :] = (u[rs] * (sv + bias_ref[...])).astype(BF16)


def _inproj_kernel(x_ref, g_ref, wn_ref, wt_ref, sg_ref, sb_ref, sw_ref, sbias_ref,
                   k_ref, hb_ref, hc_ref, yd_ref, qt_ref, vt_ref):
    gw = GROUP_WIDTH
    x = x_ref[0]
    ms = jnp.mean(x * x, axis=-1, keepdims=True)
    xn = (x * lax.rsqrt(ms + EPS) * g_ref[...]).astype(BF16)
    nat = jnp.dot(xn, wn_ref[...], preferred_element_type=F32)
    tr = lax.dot_general(wt_ref[...], xn, (((1,), (1,)), ((), ())),
                         preferred_element_type=F32)
    hw = 2 * ATT_QK
    kz = jnp.zeros((TM, KPAD - hw), F32)
    k_ref[0] = jnp.concatenate(
        [piece for h in range(ATT_HEADS) for piece in (nat[:, h * hw:(h + 1) * hw], kz)],
        axis=1).astype(BF16)
    hb_ref[0] = nat[:, gw:2 * gw] * jax.nn.sigmoid(nat[:, 2 * gw:3 * gw])
    hc_ref[0] = nat[:, 3 * gw:4 * gw].astype(BF16)
    _sgu(nat[:, 4 * gw:5 * gw], nat[:, 5 * gw:6 * gw], sg_ref, sb_ref, sw_ref, sbias_ref, yd_ref)
    qscale = (ATT_QK ** -0.5) * LOG2E
    qz = jnp.zeros((KPAD - hw, TM), F32)
    q = jnp.concatenate(
        [piece for h in range(ATT_HEADS) for piece in (tr[h * hw:(h + 1) * hw] * qscale, qz)],
        axis=0).astype(BF16)
    for j in range(TM // TQ):
        qt_ref[0, j] = q[:, j * TQ:(j + 1) * TQ]
    for j in range(TM // TK):
        vt_ref[0, j] = tr[gw:, j * TK:(j + 1) * TK].astype(BF16)


def _inproj(x, g, wn, wt, sgu_g, sgu_b, sgu_w, sgu_bias):
    b, s, d = x.shape
    kw = ATT_HEADS * KPAD
    gw = GROUP_WIDTH
    grid = (b, s // TM)
    tok = lambda w: pl.BlockSpec((1, TM, w), lambda i, j: (i, j, 0))
    out_shape = (
        jax.ShapeDtypeStruct((b, s, kw), BF16),
        jax.ShapeDtypeStruct((b, s, gw), F32),
        jax.ShapeDtypeStruct((b, s, gw), BF16),
        jax.ShapeDtypeStruct((b, s, gw), BF16),
        jax.ShapeDtypeStruct((b, s // TQ, kw, TQ), BF16),
        jax.ShapeDtypeStruct((b, s // TK, gw, TK), BF16),
    )
    out_specs = (
        tok(kw), tok(gw), tok(gw), tok(gw),
        pl.BlockSpec((1, TM // TQ, kw, TQ), lambda i, j: (i, j, 0, 0)),
        pl.BlockSpec((1, TM // TK, gw, TK), lambda i, j: (i, j, 0, 0)),
    )
    consts = (g, wn, wt, sgu_g, sgu_b, sgu_w, sgu_bias)
    return pl.pallas_call(
        _inproj_kernel,
        out_shape=out_shape,
        grid=grid,
        in_specs=[tok(d)] + [_full(c.shape) for c in consts],
        out_specs=out_specs,
        compiler_params=_params(("parallel", "parallel")),
        name="inproj",
    )(x, *consts)


def _attn_kernel(lam_ref, g_ref, crow_ref, posk_ref, posq_ref, dbias_ref,
                 qt_ref, k_ref, vt_ref, o_ref, kaug_ref, vaug_ref, rhs_ref, sa_ref, sb_ref):
    nq = qt_ref.shape[1]
    nk = vt_ref.shape[1]
    ns = Q_STREAMS

    for j in range(nk):
        kaug_ref[j] = k_ref[0, j * TK:(j + 1) * TK, :] + posk_ref[0]
        vaug_ref[j, :ATT_VDIM, :] = vt_ref[0, j]
        r = lax.broadcasted_iota(jnp.int32, (V_ROWS - ATT_VDIM, TK), 0)
        vaug_ref[j, ATT_VDIM:, :] = jnp.where(r == 0, 1.0, 0.0).astype(BF16)

    lp = lam_ref[...]
    lam_init = lp[4:5, 0:1]
    lam = (jnp.exp(jnp.sum(lp[0:1] * lp[1:2], axis=-1, keepdims=True))
           - jnp.exp(jnp.sum(lp[2:3] * lp[3:4], axis=-1, keepdims=True)) + lam_init)

    crow = crow_ref[0]
    posq = posq_ref[0]
    row = lax.broadcasted_iota(jnp.int32, (KPAD, TQ), 0)

    def key_block(u, qi):
        return jnp.where(u == 0, qi, u - 1 + jnp.where(u - 1 >= qi, 1, 0))

    def scores(buf, g, kj, rhs, diag):
        s = jnp.dot(kaug_ref[kj], rhs, preferred_element_type=F32)
        if diag:
            s = s + dbias_ref[0]
        buf[g] = s
        return jnp.max(s, axis=0, keepdims=True)

    def consume(buf, g, kj, qi, mt, mrun, acc):
        off = crow * (-lax.convert_element_type(jnp.abs(kj - qi) * TK, F32))
        mnew = jnp.maximum(mrun, mt + off)
        alpha = jnp.exp2(mrun - mnew)
        p = jnp.exp2(buf[g] - (mnew - off)).astype(BF16)
        acc = alpha * acc + jnp.dot(vaug_ref[kj], p, preferred_element_type=F32)
        return mnew, acc

    def prologue(qg):
        mts = []
        for g in range(ns):
            qi = qg * ns + g
            qt = qt_ref[0, qi]
            zero = jnp.zeros_like(qt)
            qm = [jnp.where((row >= ATT_QK * m) & (row < ATT_QK * (m + 1)), qt, zero)
                  for m in range(2)]
            rhs_ref[2 * g + 0] = jnp.concatenate([q + posq for q in qm], axis=1)
            rhs_ref[2 * g + 1] = jnp.concatenate([q - posq for q in qm], axis=1)
            mts.append(scores(sa_ref, g, qi, jnp.concatenate(qm, axis=1), True))
        return tuple(mts)

    def produce(buf, qg, u):
        out = []
        for g in range(ns):
            qi = qg * ns + g
            kj = key_block(u, qi)
            side = jnp.where(kj > qi, 1, 0)
            out.append(scores(buf, g, kj, rhs_ref[2 * g + side], False))
        return tuple(out)

    def finish(buf, qg, u, mts, mruns, accs):
        new_mruns, new_accs = [], []
        for g in range(ns):
            qi = qg * ns + g
            mr, ac = consume(buf, g, key_block(u, qi), qi, mts[g], mruns[g], accs[g])
            new_mruns.append(mr)
            new_accs.append(ac)
        return tuple(new_mruns), tuple(new_accs)

    def run_items(qg, first, count, mts, mruns, accs):
        for j in range(count):
            buf, prev = (sb_ref, sa_ref) if j % 2 == 0 else (sa_ref, sb_ref)
            new_mts = produce(buf, qg, first + j)
            mruns, accs = finish(prev, qg, first + j - 1, mts, mruns, accs)
            mts = new_mts
        return mts, mruns, accs

    n_loop = (nk - 1) // ITEMS_PER_TRIP
    n_tail = nk - 1 - n_loop * ITEMS_PER_TRIP

    def qgroup(qg, mts):
        init = (mts,
                tuple(jnp.full((1, 2 * TQ), -1e30, F32) for _ in range(ns)),
                tuple(jnp.zeros((V_ROWS, 2 * TQ), F32) for _ in range(ns)))

        def trip(t, carry):
            return run_items(qg, ITEMS_PER_TRIP * t + 1, ITEMS_PER_TRIP, *carry)

        carry = lax.fori_loop(0, n_loop, trip, init)
        mts, mruns, accs = run_items(qg, n_loop * ITEMS_PER_TRIP + 1, n_tail, *carry)
        last = sb_ref if (nk - 1) % 2 == 1 else sa_ref
        mruns, accs = finish(last, qg, nk - 1, mts, mruns, accs)
        nxt = prologue(jnp.minimum(qg + 1, nq // ns - 1))
        for g in range(ns):
            a = accs[g]
            outs = [a[:ATT_VDIM, m * TQ:(m + 1) * TQ] / a[ATT_VDIM:ATT_VDIM + 1, m * TQ:(m + 1) * TQ]
                    for m in range(2)]
            o = outs[0] - lam * outs[1]
            ms = jnp.mean(o * o, axis=0, keepdims=True)
            o_ref[0, qg * ns + g] = (o * lax.rsqrt(ms + EPS) * g_ref[...]).astype(BF16)
        return nxt

    lax.fori_loop(0, nq // ns, qgroup, prologue(0), unroll=2)


def _attention(lam_pack, gcol, consts, qt, k, vt):
    b, nq = qt.shape[0], qt.shape[1]
    nk = vt.shape[1]
    s = k.shape[1]
    crow, posk, posq, dbias = consts
    head = lambda shape: pl.BlockSpec((1,) + shape, lambda i, h: (h, 0, 0))
    return pl.pallas_call(
        _attn_kernel,
        out_shape=jax.ShapeDtypeStruct((b, nq, GROUP_WIDTH, TQ), BF16),
        grid=(b, ATT_HEADS),
        in_specs=[
            _full(lam_pack.shape), _full(gcol.shape),
            head((1, 2 * TQ)), head((TK, KPAD)), head((KPAD, TQ)), head((TK, 2 * TQ)),
            pl.BlockSpec((1, nq, KPAD, TQ), lambda i, h: (i, 0, h, 0)),
            pl.BlockSpec((1, s, KPAD), lambda i, h: (i, 0, h)),
            pl.BlockSpec((1, nk, ATT_VDIM, TK), lambda i, h: (i, 0, h, 0)),
        ],
        out_specs=pl.BlockSpec((1, nq, ATT_VDIM, TQ), lambda i, h: (i, 0, h, 0)),
        scratch_shapes=[pltpu.VMEM((nk, TK, KPAD), BF16), pltpu.VMEM((nk, V_ROWS, TK), BF16),
                        pltpu.VMEM((2 * Q_STREAMS, KPAD, 2 * TQ), BF16),
                        pltpu.VMEM((Q_STREAMS, TK, 2 * TQ), F32),
                        pltpu.VMEM((Q_STREAMS, TK, 2 * TQ), F32)],
        compiler_params=_params(("parallel", "parallel")),
        name="diff_attention",
    )(lam_pack, gcol, crow, posk, posq, dbias, qt, k, vt)


def _attention_consts():
    slopes = np.array([2.0 ** (-8.0 * (i + 1) / ATT_HEADS) for i in range(ATT_HEADS)], np.float64)
    c = (slopes * LOG2E).astype(np.float32)
    parts = []
    rem = c.astype(np.float32)
    for _ in range(N_SPLIT):
        p = rem.astype(BF16).astype(np.float32)
        parts.append(p)
        rem = (rem - p).astype(np.float32)
    rk = np.arange(TK, dtype=np.float32)
    rq = np.arange(TQ, dtype=np.float32)
    posk = np.zeros((ATT_HEADS, TK, KPAD), np.float32)
    posq = np.zeros((ATT_HEADS, KPAD, TQ), np.float32)
    for t in range(N_SPLIT):
        posk[:, :, POS_LANE + t] = rk[None, :]
        posk[:, :, POS_LANE + N_SPLIT + t] = parts[t][:, None]
        posq[:, POS_LANE + t, :] = parts[t][:, None]
        posq[:, POS_LANE + N_SPLIT + t, :] = -rq[None, :]
    dbias = -c[:, None, None] * np.abs(rq[None, None, :] - rk[None, :, None])
    dbias = np.concatenate([dbias, dbias], axis=2)
    crow = np.broadcast_to(c[:, None, None], (ATT_HEADS, 1, 2 * TQ))
    return (jnp.asarray(crow, F32), jnp.asarray(posk, BF16), jnp.asarray(posq, BF16),
            jnp.asarray(dbias, F32))


CONV_PAD = 16
CONV_ROWS = 256


def _split_dot(x, w):
    hi = x.astype(BF16)
    lo = (x - hi.astype(F32)).astype(BF16)
    return (jnp.dot(hi, w, preferred_element_type=F32)
            + jnp.dot(lo, w, preferred_element_type=F32))


def _conv_kernel(hb_ref, dww_ref, dwb_ref, lng_ref, lnb_ref, gavg_ref, pww_ref, pwb_ref,
                 o_ref, z_ref, sh_ref):
    s = hb_ref.shape[1]
    half = CONV_WIDTH // 2
    zeros = jnp.zeros((CONV_PAD, GROUP_WIDTH), F32)
    z_ref[0:CONV_PAD, :] = zeros
    z_ref[CONV_PAD + s:CONV_PAD + s + CONV_PAD, :] = zeros

    def glu(i, _):
        r0 = pl.multiple_of(i * CONV_ROWS, CONV_ROWS)
        z_ref[pl.ds(CONV_PAD + r0, CONV_ROWS), :] = hb_ref[0, pl.ds(r0, CONV_ROWS), :]
        return 0

    lax.fori_loop(0, s // CONV_ROWS, glu, 0)

    gavg = gavg_ref[...]

    def chunk(i, _):
        r0 = pl.multiple_of(i * CONV_ROWS, CONV_ROWS)
        acc = jnp.zeros((CONV_ROWS, GROUP_WIDTH), F32) + dwb_ref[...]
        win = z_ref[pl.ds(r0, CONV_ROWS + 2 * CONV_PAD), :]
        span = CONV_ROWS + 2 * CONV_PAD - 8
        for j in range(8):
            sh_ref[j] = win[j:j + span]
            for t in range(CONV_WIDTH):
                off = CONV_PAD - half + t
                if off % 8 == j:
                    a = off - j
                    acc = acc + dww_ref[t:t + 1, :] * sh_ref[j, a:a + CONV_ROWS, :]
        mu = _split_dot(acc, gavg)
        xc = acc - mu
        var = _split_dot(xc * xc, gavg)
        y = xc * lax.rsqrt(var + EPS) * lng_ref[...] + lnb_ref[...]
        y = y * jax.nn.sigmoid(y)
        out = jnp.dot(y.astype(BF16), pww_ref[...], preferred_element_type=F32) + pwb_ref[...]
        o_ref[0, pl.ds(r0, CONV_ROWS), :] = out.astype(BF16)
        return 0

    lax.fori_loop(0, s // CONV_ROWS, chunk, 0, unroll=2)


def _conformer_conv(hb, dww, dwb, lng, lnb, gavg, pww, pwb):
    b, s, _ = hb.shape
    return pl.pallas_call(
        _conv_kernel,
        out_shape=jax.ShapeDtypeStruct((b, s, GROUP_WIDTH), BF16),
        grid=(b,),
        in_specs=[pl.BlockSpec((1, s, GROUP_WIDTH), lambda i: (i, 0, 0)),
                  _full(dww.shape), _full(dwb.shape), _full(lng.shape), _full(lnb.shape),
                  _full(gavg.shape), _full(pww.shape), _full(pwb.shape)],
        out_specs=pl.BlockSpec((1, s, GROUP_WIDTH), lambda i: (i, 0, 0)),
        scratch_shapes=[pltpu.VMEM((s + 2 * CONV_PAD, GROUP_WIDTH), F32),
                        pltpu.VMEM((8, CONV_ROWS + 2 * CONV_PAD - 8, GROUP_WIDTH), F32)],
        compiler_params=_params(("parallel",)),
        name="conformer_conv",
    )(hb, dww, dwb, lng, lnb, gavg, pww, pwb)


FNET_ROWS = 512
FNET_PAD = 8


def _fnet_kernel(c_ref, cmat_ref, m1_ref, m3_ref, twc_ref, tws_ref, w_ref, b_ref,
                 o_ref, z_ref, bb_ref, y_ref):
    s = c_ref.shape[1]
    n1 = m1_ref.shape[0] // 2
    n2 = m3_ref.shape[0]
    gw = GROUP_WIDTH
    rows = min(FNET_ROWS, s)

    nl = gw // 128
    pz, pb, py = n2 + FNET_PAD, 2 * n2 + FNET_PAD, n1 + FNET_PAD

    for i in range(s // rows):
        z = jnp.dot(c_ref[0, i * rows:(i + 1) * rows, :], cmat_ref[...],
                    preferred_element_type=F32)
        for q in range(rows // n2):
            i1 = i * (rows // n2) + q
            for p in range(2 * nl):
                z_ref[p, i1 * pz:i1 * pz + n2, :] = z[q * n2:(q + 1) * n2, p * 128:(p + 1) * 128]

    m1 = m1_ref[...]
    for j in range(n2):
        zcat = jnp.concatenate(
            [jnp.concatenate([z_ref[comp * nl + p, pl.ds(j, n1, stride=pz), :] for p in range(nl)],
                             axis=1) for comp in range(2)], axis=0)
        a = jnp.dot(m1, zcat.astype(BF16), preferred_element_type=F32)
        tc = twc_ref[j]
        ts = tws_ref[j]
        for p in range(nl):
            ar = a[:n1, p * 128:(p + 1) * 128]
            ai = a[n1:, p * 128:(p + 1) * 128]
            bb_ref[p, pl.ds(j, n1, stride=pb), :] = ar * tc + ai * ts
            bb_ref[p, pl.ds(n2 + j, n1, stride=pb), :] = ai * tc - ar * ts

    m3 = m3_ref[...]
    for k1 in range(n1):
        rhs = jnp.concatenate([bb_ref[p, k1 * pb:k1 * pb + 2 * n2, :] for p in range(nl)],
                              axis=1).astype(BF16)
        y = jnp.dot(m3, rhs, preferred_element_type=F32)
        for p in range(nl):
            y_ref[p, pl.ds(k1, n2, stride=py), :] = y[:, p * 128:(p + 1) * 128]

    norm = 1.0 / math.sqrt(s * (gw // FNET_GROUPS))
    for i in range(s // rows):
        y = jnp.concatenate(
            [jnp.concatenate([y_ref[p, k2 * py:k2 * py + n1, :] for p in range(nl)], axis=1)
             for k2 in range(i * (rows // n1), (i + 1) * (rows // n1))], axis=0)
        out = jnp.dot((y * norm).astype(BF16), w_ref[...], preferred_element_type=F32) + b_ref[...]
        o_ref[0, i * rows:(i + 1) * rows, :] = out.astype(BF16)


def _fnet_consts(s):
    n1 = 1 << (int(math.log2(s)) // 2)
    n2 = s // n1
    cw = GROUP_WIDTH // FNET_GROUPS
    ang = 2.0 * np.pi * np.outer(np.arange(cw), np.arange(cw)) / cw
    eye = np.eye(FNET_GROUPS)
    cmat = np.concatenate([np.kron(eye, np.cos(ang)), -np.kron(eye, np.sin(ang))], axis=1)
    a1 = 2.0 * np.pi * np.outer(np.arange(n1), np.arange(n1)) / n1
    m1 = np.block([[np.cos(a1), np.sin(a1)], [-np.sin(a1), np.cos(a1)]])
    a2 = 2.0 * np.pi * np.outer(np.arange(n2), np.arange(n2)) / n2
    m3 = np.concatenate([np.cos(a2), np.sin(a2)], axis=1)
    at = 2.0 * np.pi * np.outer(np.arange(n2), np.arange(n1)) / s
    twc = np.broadcast_to(np.cos(at)[:, :, None], (n2, n1, 128))
    tws = np.broadcast_to(np.sin(at)[:, :, None], (n2, n1, 128))
    return (jnp.asarray(cmat, BF16), jnp.asarray(m1, BF16), jnp.asarray(m3, BF16),
            jnp.asarray(twc, F32), jnp.asarray(tws, F32))


def _fourier_mix(c, consts, w_bd, bias):
    b, s, gw = c.shape
    cmat, m1, m3, twc, tws = consts
    n1, n2 = m1.shape[0] // 2, m3.shape[0]
    return pl.pallas_call(
        _fnet_kernel,
        out_shape=jax.ShapeDtypeStruct((b, s, gw), BF16),
        grid=(b,),
        in_specs=[pl.BlockSpec((1, s, gw), lambda i: (i, 0, 0)),
                  _full(cmat.shape), _full(m1.shape), _full(m3.shape),
                  _full(twc.shape), _full(tws.shape), _full(w_bd.shape), _full(bias.shape)],
        out_specs=pl.BlockSpec((1, s, gw), lambda i: (i, 0, 0)),
        scratch_shapes=[pltpu.VMEM((2 * gw // 128, n1 * (n2 + FNET_PAD), 128), F32),
                        pltpu.VMEM((gw // 128, n1 * (2 * n2 + FNET_PAD), 128), F32),
                        pltpu.VMEM((gw // 128, n2 * (n1 + FNET_PAD), 128), F32)],
        compiler_params=_params(("parallel",)),
        name="fourier_mix",
    )(c, cmat, m1, m3, twc, tws, w_bd, bias)


FF_CHUNK = 1024


def _channel_kernel(x_ref, ya_ref, yb_ref, yc_ref, yd_ref, wo_ref, g_ref, wu_ref, wd_ref, gf_ref,
                    o_ref, xm_ref, *, final):
    phase = pl.program_id(2)

    @pl.when(phase == 0)
    def _():
        acc = (x_ref[0]
               + jnp.dot(yb_ref[0], wo_ref[1], preferred_element_type=F32)
               + jnp.dot(yc_ref[0], wo_ref[2], preferred_element_type=F32)
               + jnp.dot(yd_ref[0], wo_ref[3], preferred_element_type=F32))
        for j in range(TM // TQ):
            ya = lax.dot_general(ya_ref[0, j], wo_ref[0], (((0,), (0,)), ((), ())),
                                 preferred_element_type=F32)
            xm_ref[j * TQ:(j + 1) * TQ, :] = acc[j * TQ:(j + 1) * TQ] + ya

    @pl.when(phase == 1)
    def _():
        x = xm_ref[...]
        ms = jnp.mean(x * x, axis=-1, keepdims=True)
        xn = (x * lax.rsqrt(ms + EPS) * g_ref[...]).astype(BF16)
        acc = x
        for c in range(D_FF // FF_CHUNK):
            cs = slice(c * FF_CHUNK, (c + 1) * FF_CHUNK)
            h = jnp.maximum(jnp.dot(xn, wu_ref[:, cs], preferred_element_type=F32), 0.0)
            acc = acc + jnp.dot((h * h).astype(BF16), wd_ref[cs, :], preferred_element_type=F32)
        if final:
            ms = jnp.mean(acc * acc, axis=-1, keepdims=True)
            acc = acc * lax.rsqrt(ms + EPS) * gf_ref[...]
        o_ref[0] = acc


def _channel_mix(x, ya_t, yb, yc, yd, wo, g, wu, wd, gf, final):
    b, s, d = x.shape
    gw = GROUP_WIDTH
    tok = lambda width: pl.BlockSpec((1, TM, width), lambda i, j, p: (i, j, 0))
    const = lambda shape: pl.BlockSpec(shape, lambda i, j, p: (0,) * len(shape),
                                       pipeline_mode=pl.Buffered(1))
    return pl.pallas_call(
        functools.partial(_channel_kernel, final=final),
        out_shape=jax.ShapeDtypeStruct((b, s, d), F32),
        grid=(b, s // TM, 2),
        in_specs=[tok(d), pl.BlockSpec((1, TM // TQ, gw, TQ), lambda i, j, p: (i, j, 0, 0)),
                  tok(gw), tok(gw), tok(gw), const(wo.shape), const(g.shape),
                  const(wu.shape), const(wd.shape), const(gf.shape)],
        out_specs=tok(d),
        scratch_shapes=[pltpu.VMEM((TM, d), F32)],
        compiler_params=_params(("parallel", "parallel", "arbitrary")),
        name="channel_mix",
    )(x, ya_t, yb, yc, yd, wo, g, wu, wd, gf)


def _block_diag(blocks):
    g, r, c = blocks.shape
    eye = jnp.eye(g, dtype=blocks.dtype)
    return (eye[:, None, :, None] * blocks[:, :, None, :]).reshape(g * r, g * c)


def _layer_weights(w_in_l):
    gw = GROUP_WIDTH
    wn = w_in_l[:, gw:2 * gw]
    wn = jnp.concatenate([wn, w_in_l[:, 3 * gw:]], axis=1).astype(BF16)
    wt = jnp.concatenate([w_in_l[:, :gw], w_in_l[:, 2 * gw:3 * gw]], axis=1).T.astype(BF16)
    return wn, wt


def kernel(x, norm1_g, w_in, lam_q1, lam_k1, lam_q2, lam_k2, subln_g, conv_dw_w, conv_dw_b, conv_ln_g, conv_ln_b, conv_pw_w, conv_pw_b, fnet_w, fnet_b, sgu_ln_g, sgu_ln_b, sgu_w, sgu_b, w_out, norm2_g, w_up, w_down, final_g):
    bsz, seq, d = x.shape
    depth = w_in.shape[0]
    gw = GROUP_WIDTH
    assert seq % TM == 0 and d == D_MODEL

    attn_consts = _attention_consts()
    fnet_consts = _fnet_consts(seq)
    cg = gw // CONV_GROUPS
    gavg = jnp.asarray(np.kron(np.eye(CONV_GROUPS), np.full((cg, cg), 1.0 / cg)), BF16)
    row = lambda v: v.reshape(1, -1).astype(F32)

    for l in range(depth):
        wn, wt = _layer_weights(w_in[l])
        wcat = sgu_w[l].reshape(SGU_GROUPS * SGU_CHUNK, SGU_CHUNK).astype(BF16)
        bias_plane = jnp.repeat(sgu_b[l].T.astype(F32), gw // SGU_GROUPS, axis=1)
        k, hb, hc, yd, qt, vt = _inproj(x, row(norm1_g[l]), wn, wt, row(sgu_ln_g[l]),
                                        row(sgu_ln_b[l]), wcat, bias_plane)

        lam_init = 0.8 - 0.6 * math.exp(-0.3 * l)
        lam_pack = jnp.concatenate(
            [lam_q1[l][None], lam_k1[l][None], lam_q2[l][None], lam_k2[l][None],
             jnp.full((4, ATT_QK), lam_init, F32)], axis=0).astype(F32)
        gcol = (subln_g[l].astype(F32) * (1.0 - lam_init)).reshape(ATT_VDIM, 1)
        ya_t = _attention(lam_pack, gcol, attn_consts, qt, k, vt)

        yb = _conformer_conv(hb, conv_dw_w[l].astype(F32), row(conv_dw_b[l]), row(conv_ln_g[l]),
                             row(conv_ln_b[l]), gavg, conv_pw_w[l].astype(BF16), row(conv_pw_b[l]))

        yc = _fourier_mix(hc, fnet_consts, _block_diag(fnet_w[l]).astype(BF16), row(fnet_b[l]))

        x = _channel_mix(x, ya_t, yb, yc, yd, w_out[l].reshape(4, gw, d).astype(BF16),
                         row(norm2_g[l]), w_up[l].astype(BF16), w_down[l].astype(BF16),
                         row(final_g), l == depth - 1)

    return x
```

```python
import functools
import math

import numpy as np
import jax
import jax.numpy as jnp
from jax import lax
from jax.experimental import pallas as pl
from jax.experimental.pallas import tpu as pltpu

D_MODEL = 1024
GROUP_WIDTH = 256
ATT_HEADS = 4
ATT_VDIM = 64
ATT_QK = 32
CONV_WIDTH = 31
CONV_GROUPS = 4
FNET_GROUPS = 4
SGU_CHUNK = 128
SGU_GROUPS = 4
D_FF = 4 * D_MODEL
EPS = 1e-6
LOG2E = 1.4426950408889634

KPAD = 128
POS_LANE = 2 * ATT_QK
N_SPLIT = 3
TQ = 256
TK = 256
V_ROWS = ATT_VDIM + 16
Q_STREAMS = 2
ITEMS_PER_TRIP = 16
TM = 512
VMEM_LIMIT = 56 * 1024 * 1024

BF16 = jnp.bfloat16
F32 = jnp.float32


def _params(sem, vmem=VMEM_LIMIT):
    return pltpu.CompilerParams(dimension_semantics=sem, vmem_limit_bytes=vmem)


def _full(shape):
    n = len(shape)
    return pl.BlockSpec(shape, lambda *_: (0,) * n)


def _sgu(u, v, g_ref, b_ref, w_ref, bias_ref, o_ref):
    mu = jnp.mean(v, axis=-1, keepdims=True)
    xc = v - mu
    var = jnp.mean(xc * xc, axis=-1, keepdims=True)
    vn = (xc * lax.rsqrt(var + EPS) * g_ref[...] + b_ref[...]).astype(BF16)
    cg = GROUP_WIDTH // SGU_GROUPS
    lane = lax.broadcasted_iota(jnp.int32, (SGU_CHUNK, GROUP_WIDTH), 1)
    w = w_ref[...]
    for c in range(TM // SGU_CHUNK):
        rs = slice(c * SGU_CHUNK, (c + 1) * SGU_CHUNK)
        r = jnp.dot(w, vn[rs], preferred_element_type=F32)
        sv = r[(SGU_GROUPS - 1) * SGU_CHUNK:]
        for g in range(SGU_GROUPS - 2, -1, -1):
            sv = jnp.where(lane < (g + 1) * cg, r[g * SGU_CHUNK:(g + 1) * SGU_CHUNK], sv)
        o_ref[0, rs, :] = (u[rs] * (sv + bias_ref[...])).astype(BF16)


def _inproj_kernel(x_ref, g_ref, wn_ref, wt_ref, sg_ref, sb_ref, sw_ref, sbias_ref,
                   k_ref, hb_ref, hc_ref, yd_ref, qt_ref, vt_ref):
    gw = GROUP_WIDTH
    x = x_ref[0]
    ms = jnp.mean(x * x, axis=-1, keepdims=True)
    xn = (x * lax.rsqrt(ms + EPS) * g_ref[...]).astype(BF16)
    nat = jnp.dot(xn, wn_ref[...], preferred_element_type=F32)
    tr = lax.dot_general(wt_ref[...], xn, (((1,), (1,)), ((), ())),
                         preferred_element_type=F32)
    hw = 2 * ATT_QK
    kz = jnp.zeros((TM, KPAD - hw), F32)
    k_ref[0] = jnp.concatenate(
        [piece for h in range(ATT_HEADS) for piece in (nat[:, h * hw:(h + 1) * hw], kz)],
        axis=1).astype(BF16)
    hb_ref[0] = nat[:, gw:2 * gw] * jax.nn.sigmoid(nat[:, 2 * gw:3 * gw])
    hc_ref[0] = nat[:, 3 * gw:4 * gw].astype(BF16)
    _sgu(nat[:, 4 * gw:5 * gw], nat[:, 5 * gw:6 * gw], sg_ref, sb_ref, sw_ref, sbias_ref, yd_ref)
    qscale = (ATT_QK ** -0.5) * LOG2E
    qz = jnp.zeros((KPAD - hw, TM), F32)
    q = jnp.concatenate(
        [piece for h in range(ATT_HEADS) for piece in (tr[h * hw:(h + 1) * hw] * qscale, qz)],
        axis=0).astype(BF16)
    for j in range(TM // TQ):
        qt_ref[0, j] = q[:, j * TQ:(j + 1) * TQ]
    for j in range(TM // TK):
        vt_ref[0, j] = tr[gw:, j * TK:(j + 1) * TK].astype(BF16)


def _inproj(x, g, wn, wt, sgu_g, sgu_b, sgu_w, sgu_bias):
    b, s, d = x.shape
    kw = ATT_HEADS * KPAD
    gw = GROUP_WIDTH
    grid = (b, s // TM)
    tok = lambda w: pl.BlockSpec((1, TM, w), lambda i, j: (i, j, 0))
    out_shape = (
        jax.ShapeDtypeStruct((b, s, kw), BF16),
        jax.ShapeDtypeStruct((b, s, gw), F32),
        jax.ShapeDtypeStruct((b, s, gw), BF16),
        jax.ShapeDtypeStruct((b, s, gw), BF16),
        jax.ShapeDtypeStruct((b, s // TQ, kw, TQ), BF16),
        jax.ShapeDtypeStruct((b, s // TK, gw, TK), BF16),
    )
    out_specs = (
        tok(kw), tok(gw), tok(gw), tok(gw),
        pl.BlockSpec((1, TM // TQ, kw, TQ), lambda i, j: (i, j, 0, 0)),
        pl.BlockSpec((1, TM // TK, gw, TK), lambda i, j: (i, j, 0, 0)),
    )
    consts = (g, wn, wt, sgu_g, sgu_b, sgu_w, sgu_bias)
    return pl.pallas_call(
        _inproj_kernel,
        out_shape=out_shape,
        grid=grid,
        in_specs=[tok(d)] + [_full(c.shape) for c in consts],
        out_specs=out_specs,
        compiler_params=_params(("parallel", "parallel")),
        name="inproj",
    )(x, *consts)


def _attn_kernel(lam_ref, g_ref, crow_ref, posk_ref, posq_ref, dbias_ref,
                 qt_ref, k_ref, vt_ref, o_ref, kaug_ref, vaug_ref, rhs_ref, sa_ref, sb_ref):
    nq = qt_ref.shape[1]
    nk = vt_ref.shape[1]
    ns = Q_STREAMS

    for j in range(nk):
        kaug_ref[j] = k_ref[0, j * TK:(j + 1) * TK, :] + posk_ref[0]
        vaug_ref[j, :ATT_VDIM, :] = vt_ref[0, j]
        r = lax.broadcasted_iota(jnp.int32, (V_ROWS - ATT_VDIM, TK), 0)
        vaug_ref[j, ATT_VDIM:, :] = jnp.where(r == 0, 1.0, 0.0).astype(BF16)

    lp = lam_ref[...]
    lam_init = lp[4:5, 0:1]
    lam = (jnp.exp(jnp.sum(lp[0:1] * lp[1:2], axis=-1, keepdims=True))
           - jnp.exp(jnp.sum(lp[2:3] * lp[3:4], axis=-1, keepdims=True)) + lam_init)

    crow = crow_ref[0]
    posq = posq_ref[0]
    row = lax.broadcasted_iota(jnp.int32, (KPAD, TQ), 0)

    def key_block(u, qi):
        return jnp.where(u == 0, qi, u - 1 + jnp.where(u - 1 >= qi, 1, 0))

    def scores(buf, g, kj, rhs, diag):
        s = jnp.dot(kaug_ref[kj], rhs, preferred_element_type=F32)
        if diag:
            s = s + dbias_ref[0]
        buf[g] = s
        return jnp.max(s, axis=0, keepdims=True)

    def consume(buf, g, kj, qi, mt, mrun, acc):
        off = crow * (-lax.convert_element_type(jnp.abs(kj - qi) * TK, F32))
        mnew = jnp.maximum(mrun, mt + off)
        alpha = jnp.exp2(mrun - mnew)
        p = jnp.exp2(buf[g] - (mnew - off)).astype(BF16)
        acc = alpha * acc + jnp.dot(vaug_ref[kj], p, preferred_element_type=F32)
        return mnew, acc

    def prologue(qg):
        mts = []
        for g in range(ns):
            qi = qg * ns + g
            qt = qt_ref[0, qi]
            zero = jnp.zeros_like(qt)
            qm = [jnp.where((row >= ATT_QK * m) & (row < ATT_QK * (m + 1)), qt, zero)
                  for m in range(2)]
            rhs_ref[2 * g + 0] = jnp.concatenate([q + posq for q in qm], axis=1)
            rhs_ref[2 * g + 1] = jnp.concatenate([q - posq for q in qm], axis=1)
            mts.append(scores(sa_ref, g, qi, jnp.concatenate(qm, axis=1), True))
        return tuple(mts)

    def produce(buf, qg, u):
        out = []
        for g in range(ns):
            qi = qg * ns + g
            kj = key_block(u, qi)
            side = jnp.where(kj > qi, 1, 0)
            out.append(scores(buf, g, kj, rhs_ref[2 * g + side], False))
        return tuple(out)

    def finish(buf, qg, u, mts, mruns, accs):
        new_mruns, new_accs = [], []
        for g in range(ns):
            qi = qg * ns + g
            mr, ac = consume(buf, g, key_block(u, qi), qi, mts[g], mruns[g], accs[g])
            new_mruns.append(mr)
            new_accs.append(ac)
        return tuple(new_mruns), tuple(new_accs)

    def run_items(qg, first, count, mts, mruns, accs):
        for j in range(count):
            buf, prev = (sb_ref, sa_ref) if j % 2 == 0 else (sa_ref, sb_ref)
            new_mts = produce(buf, qg, first + j)
            mruns, accs = finish(prev, qg, first + j - 1, mts, mruns, accs)
            mts = new_mts
        return mts, mruns, accs

    n_loop = (nk - 1) // ITEMS_PER_TRIP
    n_tail = nk - 1 - n_loop * ITEMS_PER_TRIP

    def qgroup(qg, mts):
        init = (mts,
                tuple(jnp.full((1, 2 * TQ), -1e30, F32) for _ in range(ns)),
                tuple(jnp.zeros((V_ROWS, 2 * TQ), F32) for _ in range(ns)))

        def trip(t, carry):
            return run_items(qg, ITEMS_PER_TRIP * t + 1, ITEMS_PER_TRIP, *carry)

        carry = lax.fori_loop(0, n_loop, trip, init)
        mts, mruns, accs = run_items(qg, n_loop * ITEMS_PER_TRIP + 1, n_tail, *carry)
        last = sb_ref if (nk - 1) % 2 == 1 else sa_ref
        mruns, accs = finish(last, qg, nk - 1, mts, mruns, accs)
        nxt = prologue(jnp.minimum(qg + 1, nq // ns - 1))
        for g in range(ns):
            a = accs[g]
            outs = [a[:ATT_VDIM, m * TQ:(m + 1) * TQ] / a[ATT_VDIM:ATT_VDIM + 1, m * TQ:(m + 1) * TQ]
                    for m in range(2)]
            o = outs[0] - lam * outs[1]
            ms = jnp.mean(o * o, axis=0, keepdims=True)
            o_ref[0, qg * ns + g] = (o * lax.rsqrt(ms + EPS) * g_ref[...]).astype(BF16)
        return nxt

    lax.fori_loop(0, nq // ns, qgroup, prologue(0), unroll=2)


def _attention(lam_pack, gcol, consts, qt, k, vt):
    b, nq = qt.shape[0], qt.shape[1]
    nk = vt.shape[1]
    s = k.shape[1]
    crow, posk, posq, dbias = consts
    head = lambda shape: pl.BlockSpec((1,) + shape, lambda i, h: (h, 0, 0))
    return pl.pallas_call(
        _attn_kernel,
        out_shape=jax.ShapeDtypeStruct((b, nq, GROUP_WIDTH, TQ), BF16),
        grid=(b, ATT_HEADS),
        in_specs=[
            _full(lam_pack.shape), _full(gcol.shape),
            head((1, 2 * TQ)), head((TK, KPAD)), head((KPAD, TQ)), head((TK, 2 * TQ)),
            pl.BlockSpec((1, nq, KPAD, TQ), lambda i, h: (i, 0, h, 0)),
            pl.BlockSpec((1, s, KPAD), lambda i, h: (i, 0, h)),
            pl.BlockSpec((1, nk, ATT_VDIM, TK), lambda i, h: (i, 0, h, 0)),
        ],
        out_specs=pl.BlockSpec((1, nq, ATT_VDIM, TQ), lambda i, h: (i, 0, h, 0)),
        scratch_shapes=[pltpu.VMEM((nk, TK, KPAD), BF16), pltpu.VMEM((nk, V_ROWS, TK), BF16),
                        pltpu.VMEM((2 * Q_STREAMS, KPAD, 2 * TQ), BF16),
                        pltpu.VMEM((Q_STREAMS, TK, 2 * TQ), F32),
                        pltpu.VMEM((Q_STREAMS, TK, 2 * TQ), F32)],
        compiler_params=_params(("parallel", "parallel")),
        name="diff_attention",
    )(lam_pack, gcol, crow, posk, posq, dbias, qt, k, vt)


def _attention_consts():
    slopes = np.array([2.0 ** (-8.0 * (i + 1) / ATT_HEADS) for i in range(ATT_HEADS)], np.float64)
    c = (slopes * LOG2E).astype(np.float32)
    parts = []
    rem = c.astype(np.float32)
    for _ in range(N_SPLIT):
        p = rem.astype(BF16).astype(np.float32)
        parts.append(p)
        rem = (rem - p).astype(np.float32)
    rk = np.arange(TK, dtype=np.float32)
    rq = np.arange(TQ, dtype=np.float32)
    posk = np.zeros((ATT_HEADS, TK, KPAD), np.float32)
    posq = np.zeros((ATT_HEADS, KPAD, TQ), np.float32)
    for t in range(N_SPLIT):
        posk[:, :, POS_LANE + t] = rk[None, :]
        posk[:, :, POS_LANE + N_SPLIT + t] = parts[t][:, None]
        posq[:, POS_LANE + t, :] = parts[t][:, None]
        posq[:, POS_LANE + N_SPLIT + t, :] = -rq[None, :]
    dbias = -c[:, None, None] * np.abs(rq[None, None, :] - rk[None, :, None])
    dbias = np.concatenate([dbias, dbias], axis=2)
    crow = np.broadcast_to(c[:, None, None], (ATT_HEADS, 1, 2 * TQ))
    return (jnp.asarray(crow, F32), jnp.asarray(posk, BF16), jnp.asarray(posq, BF16),
            jnp.asarray(dbias, F32))


CONV_PAD = 16
CONV_ROWS = 256


def _split_dot(x, w):
    hi = x.astype(BF16)
    lo = (x - hi.astype(F32)).astype(BF16)
    return (jnp.dot(hi, w, preferred_element_type=F32)
            + jnp.dot(lo, w, preferred_element_type=F32))


def _conv_kernel(hb_ref, dww_ref, dwb_ref, lng_ref, lnb_ref, gavg_ref, pww_ref, pwb_ref,
                 o_ref, z_ref, sh_ref):
    s = hb_ref.shape[1]
    half = CONV_WIDTH // 2
    zeros = jnp.zeros((CONV_PAD, GROUP_WIDTH), F32)
    z_ref[0:CONV_PAD, :] = zeros
    z_ref[CONV_PAD + s:CONV_PAD + s + CONV_PAD, :] = zeros

    def glu(i, _):
        r0 = pl.multiple_of(i * CONV_ROWS, CONV_ROWS)
        z_ref[pl.ds(CONV_PAD + r0, CONV_ROWS), :] = hb_ref[0, pl.ds(r0, CONV_ROWS), :]
        return 0

    lax.fori_loop(0, s // CONV_ROWS, glu, 0)

    gavg = gavg_ref[...]

    def chunk(i, _):
        r0 = pl.multiple_of(i * CONV_ROWS, CONV_ROWS)
        acc = jnp.zeros((CONV_ROWS, GROUP_WIDTH), F32) + dwb_ref[...]
        win = z_ref[pl.ds(r0, CONV_ROWS + 2 * CONV_PAD), :]
        span = CONV_ROWS + 2 * CONV_PAD - 8
        for j in range(8):
            sh_ref[j] = win[j:j + span]
            for t in range(CONV_WIDTH):
                off = CONV_PAD - half + t
                if off % 8 == j:
                    a = off - j
                    acc = acc + dww_ref[t:t + 1, :] * sh_ref[j, a:a + CONV_ROWS, :]
        mu = _split_dot(acc, gavg)
        xc = acc - mu
        var = _split_dot(xc * xc, gavg)
        y = xc * lax.rsqrt(var + EPS) * lng_ref[...] + lnb_ref[...]
        y = y * jax.nn.sigmoid(y)
        out = jnp.dot(y.astype(BF16), pww_ref[...], preferred_element_type=F32) + pwb_ref[...]
        o_ref[0, pl.ds(r0, CONV_ROWS), :] = out.astype(BF16)
        return 0

    lax.fori_loop(0, s // CONV_ROWS, chunk, 0, unroll=2)


def _conformer_conv(hb, dww, dwb, lng, lnb, gavg, pww, pwb):
    b, s, _ = hb.shape
    return pl.pallas_call(
        _conv_kernel,
        out_shape=jax.ShapeDtypeStruct((b, s, GROUP_WIDTH), BF16),
        grid=(b,),
        in_specs=[pl.BlockSpec((1, s, GROUP_WIDTH), lambda i: (i, 0, 0)),
                  _full(dww.shape), _full(dwb.shape), _full(lng.shape), _full(lnb.shape),
                  _full(gavg.shape), _full(pww.shape), _full(pwb.shape)],
        out_specs=pl.BlockSpec((1, s, GROUP_WIDTH), lambda i: (i, 0, 0)),
        scratch_shapes=[pltpu.VMEM((s + 2 * CONV_PAD, GROUP_WIDTH), F32),
                        pltpu.VMEM((8, CONV_ROWS + 2 * CONV_PAD - 8, GROUP_WIDTH), F32)],
        compiler_params=_params(("parallel",)),
        name="conformer_conv",
    )(hb, dww, dwb, lng, lnb, gavg, pww, pwb)


FNET_ROWS = 512
FNET_PAD = 4


def _fnet_kernel(c_ref, cmat_ref, m1_ref, m3_ref, twc_ref, tws_ref, w_ref, b_ref,
                 o_ref, z_ref, bb_ref, y_ref):
    s = c_ref.shape[1]
    n1 = m1_ref.shape[0] // 2
    n2 = m3_ref.shape[0]
    gw = GROUP_WIDTH
    rows = min(FNET_ROWS, s)

    nl = gw // 128
    pz, pb, py = n2 + FNET_PAD, 2 * n2 + FNET_PAD, n1 + FNET_PAD

    for i in range(s // rows):
        z = jnp.dot(c_ref[0, i * rows:(i + 1) * rows, :], cmat_ref[...],
                    preferred_element_type=F32)
        for q in range(rows // n2):
            i1 = i * (rows // n2) + q
            for p in range(2 * nl):
                z_ref[p, i1 * pz:i1 * pz + n2, :] = z[q * n2:(q + 1) * n2, p * 128:(p + 1) * 128]

    m1 = m1_ref[...]
    for j in range(n2):
        zcat = jnp.concatenate(
            [jnp.concatenate([z_ref[comp * nl + p, pl.ds(j, n1, stride=pz), :] for p in range(nl)],
                             axis=1) for comp in range(2)], axis=0)
        a = jnp.dot(m1, zcat.astype(BF16), preferred_element_type=F32)
        tc = twc_ref[j]
        ts = tws_ref[j]
        for p in range(nl):
            ar = a[:n1, p * 128:(p + 1) * 128]
            ai = a[n1:, p * 128:(p + 1) * 128]
            bb_ref[p, pl.ds(j, n1, stride=pb), :] = ar * tc + ai * ts
            bb_ref[p, pl.ds(n2 + j, n1, stride=pb), :] = ai * tc - ar * ts

    m3 = m3_ref[...]
    for k1 in range(n1):
        rhs = jnp.concatenate([bb_ref[p, k1 * pb:k1 * pb + 2 * n2, :] for p in range(nl)],
                              axis=1).astype(BF16)
        y = jnp.dot(m3, rhs, preferred_element_type=F32)
        for p in range(nl):
            y_ref[p, pl.ds(k1, n2, stride=py), :] = y[:, p * 128:(p + 1) * 128]

    norm = 1.0 / math.sqrt(s * (gw // FNET_GROUPS))
    for i in range(s // rows):
        y = jnp.concatenate(
            [jnp.concatenate([y_ref[p, k2 * py:k2 * py + n1, :] for p in range(nl)], axis=1)
             for k2 in range(i * (rows // n1), (i + 1) * (rows // n1))], axis=0)
        out = jnp.dot((y * norm).astype(BF16), w_ref[...], preferred_element_type=F32) + b_ref[...]
        o_ref[0, i * rows:(i + 1) * rows, :] = out.astype(BF16)


def _fnet_consts(s):
    n1 = 1 << (int(math.log2(s)) // 2)
    n2 = s // n1
    cw = GROUP_WIDTH // FNET_GROUPS
    ang = 2.0 * np.pi * np.outer(np.arange(cw), np.arange(cw)) / cw
    eye = np.eye(FNET_GROUPS)
    cmat = np.concatenate([np.kron(eye, np.cos(ang)), -np.kron(eye, np.sin(ang))], axis=1)
    a1 = 2.0 * np.pi * np.outer(np.arange(n1), np.arange(n1)) / n1
    m1 = np.block([[np.cos(a1), np.sin(a1)], [-np.sin(a1), np.cos(a1)]])
    a2 = 2.0 * np.pi * np.outer(np.arange(n2), np.arange(n2)) / n2
    m3 = np.concatenate([np.cos(a2), np.sin(a2)], axis=1)
    at = 2.0 * np.pi * np.outer(np.arange(n2), np.arange(n1)) / s
    twc = np.broadcast_to(np.cos(at)[:, :, None], (n2, n1, 128))
    tws = np.broadcast_to(np.sin(at)[:, :, None], (n2, n1, 128))
    return (jnp.asarray(cmat, BF16), jnp.asarray(m1, BF16), jnp.asarray(m3, BF16),
            jnp.asarray(twc, F32), jnp.asarray(tws, F32))


def _fourier_mix(c, consts, w_bd, bias):
    b, s, gw = c.shape
    cmat, m1, m3, twc, tws = consts
    n1, n2 = m1.shape[0] // 2, m3.shape[0]
    return pl.pallas_call(
        _fnet_kernel,
        out_shape=jax.ShapeDtypeStruct((b, s, gw), BF16),
        grid=(b,),
        in_specs=[pl.BlockSpec((1, s, gw), lambda i: (i, 0, 0)),
                  _full(cmat.shape), _full(m1.shape), _full(m3.shape),
                  _full(twc.shape), _full(tws.shape), _full(w_bd.shape), _full(bias.shape)],
        out_specs=pl.BlockSpec((1, s, gw), lambda i: (i, 0, 0)),
        scratch_shapes=[pltpu.VMEM((2 * gw // 128, n1 * (n2 + FNET_PAD), 128), F32),
                        pltpu.VMEM((gw // 128, n1 * (2 * n2 + FNET_PAD), 128), F32),
                        pltpu.VMEM((gw // 128, n2 * (n1 + FNET_PAD), 128), F32)],
        compiler_params=_params(("parallel",)),
        name="fourier_mix",
    )(c, cmat, m1, m3, twc, tws, w_bd, bias)


FF_CHUNK = 1024


def _channel_kernel(x_ref, ya_ref, yb_ref, yc_ref, yd_ref, wo_ref, g_ref, wu_ref, wd_ref, gf_ref,
                    o_ref, xm_ref, *, final):
    phase = pl.program_id(2)

    @pl.when(phase == 0)
    def _():
        acc = (x_ref[0]
               + jnp.dot(yb_ref[0], wo_ref[1], preferred_element_type=F32)
               + jnp.dot(yc_ref[0], wo_ref[2], preferred_element_type=F32)
               + jnp.dot(yd_ref[0], wo_ref[3], preferred_element_type=F32))
        for j in range(TM // TQ):
            ya = lax.dot_general(ya_ref[0, j], wo_ref[0], (((0,), (0,)), ((), ())),
                                 preferred_element_type=F32)
            xm_ref[j * TQ:(j + 1) * TQ, :] = acc[j * TQ:(j + 1) * TQ] + ya

    @pl.when(phase == 1)
    def _():
        x = xm_ref[...]
        ms = jnp.mean(x * x, axis=-1, keepdims=True)
        xn = (x * lax.rsqrt(ms + EPS) * g_ref[...]).astype(BF16)
        acc = x
        for c in range(D_FF // FF_CHUNK):
            cs = slice(c * FF_CHUNK, (c + 1) * FF_CHUNK)
            h = jnp.maximum(jnp.dot(xn, wu_ref[:, cs], preferred_element_type=F32), 0.0)
            acc = acc + jnp.dot((h * h).astype(BF16), wd_ref[cs, :], preferred_element_type=F32)
        if final:
            ms = jnp.mean(acc * acc, axis=-1, keepdims=True)
            acc = acc * lax.rsqrt(ms + EPS) * gf_ref[...]
        o_ref[0] = acc


def _channel_mix(x, ya_t, yb, yc, yd, wo, g, wu, wd, gf, final):
    b, s, d = x.shape
    gw = GROUP_WIDTH
    tok = lambda width: pl.BlockSpec((1, TM, width), lambda i, j, p: (i, j, 0))
    const = lambda shape: pl.BlockSpec(shape, lambda i, j, p: (0,) * len(shape),
                                       pipeline_mode=pl.Buffered(1))
    return pl.pallas_call(
        functools.partial(_channel_kernel, final=final),
        out_shape=jax.ShapeDtypeStruct((b, s, d), F32),
        grid=(b, s // TM, 2),
        in_specs=[tok(d), pl.BlockSpec((1, TM // TQ, gw, TQ), lambda i, j, p: (i, j, 0, 0)),
                  tok(gw), tok(gw), tok(gw), const(wo.shape), const(g.shape),
                  const(wu.shape), const(wd.shape), const(gf.shape)],
        out_specs=tok(d),
        scratch_shapes=[pltpu.VMEM((TM, d), F32)],
        compiler_params=_params(("parallel", "parallel", "arbitrary")),
        name="channel_mix",
    )(x, ya_t, yb, yc, yd, wo, g, wu, wd, gf)


def _block_diag(blocks):
    g, r, c = blocks.shape
    eye = jnp.eye(g, dtype=blocks.dtype)
    return (eye[:, None, :, None] * blocks[:, :, None, :]).reshape(g * r, g * c)


def _layer_weights(w_in_l):
    gw = GROUP_WIDTH
    wn = w_in_l[:, gw:2 * gw]
    wn = jnp.concatenate([wn, w_in_l[:, 3 * gw:]], axis=1).astype(BF16)
    wt = jnp.concatenate([w_in_l[:, :gw], w_in_l[:, 2 * gw:3 * gw]], axis=1).T.astype(BF16)
    return wn, wt


def kernel(x, norm1_g, w_in, lam_q1, lam_k1, lam_q2, lam_k2, subln_g, conv_dw_w, conv_dw_b, conv_ln_g, conv_ln_b, conv_pw_w, conv_pw_b, fnet_w, fnet_b, sgu_ln_g, sgu_ln_b, sgu_w, sgu_b, w_out, norm2_g, w_up, w_down, final_g):
    bsz, seq, d = x.shape
    depth = w_in.shape[0]
    gw = GROUP_WIDTH
    assert seq % TM == 0 and d == D_MODEL

    attn_consts = _attention_consts()
    fnet_consts = _fnet_consts(seq)
    cg = gw // CONV_GROUPS
    gavg = jnp.asarray(np.kron(np.eye(CONV_GROUPS), np.full((cg, cg), 1.0 / cg)), BF16)
    row = lambda v: v.reshape(1, -1).astype(F32)

    for l in range(depth):
        wn, wt = _layer_weights(w_in[l])
        wcat = sgu_w[l].reshape(SGU_GROUPS * SGU_CHUNK, SGU_CHUNK).astype(BF16)
        bias_plane = jnp.repeat(sgu_b[l].T.astype(F32), gw // SGU_GROUPS, axis=1)
        k, hb, hc, yd, qt, vt = _inproj(x, row(norm1_g[l]), wn, wt, row(sgu_ln_g[l]),
                                        row(sgu_ln_b[l]), wcat, bias_plane)

        lam_init = 0.8 - 0.6 * math.exp(-0.3 * l)
        lam_pack = jnp.concatenate(
            [lam_q1[l][None], lam_k1[l][None], lam_q2[l][None], lam_k2[l][None],
             jnp.full((4, ATT_QK), lam_init, F32)], axis=0).astype(F32)
        gcol = (subln_g[l].astype(F32) * (1.0 - lam_init)).reshape(ATT_VDIM, 1)
        ya_t = _attention(lam_pack, gcol, attn_consts, qt, k, vt)

        yb = _conformer_conv(hb, conv_dw_w[l].astype(F32), row(conv_dw_b[l]), row(conv_ln_g[l]),
                             row(conv_ln_b[l]), gavg, conv_pw_w[l].astype(BF16), row(conv_pw_b[l]))

        yc = _fourier_mix(hc, fnet_consts, _block_diag(fnet_w[l]).astype(BF16), row(fnet_b[l]))

        x = _channel_mix(x, ya_t, yb, yc, yd, w_out[l].reshape(4, gw, d).astype(BF16),
                         row(norm2_g[l]), w_up[l].astype(BF16), w_down[l].astype(BF16),
                         row(final_g), l == depth - 1)

    return x
```
